```python
import jax, jax.numpy as jnp
from jax import lax
import numpy as np

D_MODEL = 1024
BATCH = 4
SEQ = 8192
DEPTH = 2

GRID_W = 64
CTX_LEN = 256
A_HEADS = 8
A_KV_HEADS = 2
A_HEAD_DIM = 64
WINDOW = 128
A_BLOCK = 128
ROPE_BASE = 10000.0
G_HEADS = 4
G_DK = 64
G_DV = 128
G_RANK = 16
G_TAU = 16.0
M_HEADS = 4
M_HEAD_DIM = 128
M_CONV = 5
CHUNK = 64
D_FF = 2816
N_MOD = 9
EPS = 1e-6

A_Q = A_HEADS * A_HEAD_DIM
A_KV = A_KV_HEADS * A_HEAD_DIM
G_QK = G_HEADS * G_DK
G_V = G_HEADS * G_DV
M_W = M_HEADS * M_HEAD_DIM
IN_SPLITS = (A_Q, A_KV, A_KV,
             G_QK, G_QK, G_V, G_V, 2 * G_RANK,
             M_W, M_W, M_W, M_W, 2 * M_HEADS, 2 * M_HEADS,
             D_MODEL, D_MODEL, D_MODEL)
D_IN = sum(IN_SPLITS)

kernel_name = 'hybrid_diffusion_gated_parallel_mixers'

F32 = jnp.float32


def rmsnorm(x, g):
    xf = x.astype(F32)
    y = xf * lax.rsqrt(jnp.mean(xf * xf, axis=-1, keepdims=True) + EPS)
    return (y * g.astype(F32)).astype(x.dtype)


def swiglu(h, w13, w2):
    a, b = jnp.split(h @ w13, 2, axis=-1)
    return (jax.nn.silu(a) * b) @ w2


def split_cols(z):
    idx = [int(i) for i in np.cumsum(IN_SPLITS)[:-1]]
    return jnp.split(z, idx, axis=-1)


def dwconv_centred(x, w, b):
    pad = w.shape[0] // 2
    y = lax.conv_general_dilated(x, w[:, None, :].astype(x.dtype), (1,), [(pad, pad)],
                                 dimension_numbers=('NWC', 'WIO', 'NWC'),
                                 feature_group_count=x.shape[-1])
    return y + b.astype(x.dtype)


def axial_rope(rows):
    r = jnp.repeat(jnp.arange(rows), GRID_W).astype(F32)
    col = jnp.tile(jnp.arange(GRID_W), rows).astype(F32)
    n_freq = A_HEAD_DIM // 4
    inv = ROPE_BASE ** (-jnp.arange(n_freq, dtype=F32) / n_freq)
    ang = jnp.concatenate([r[:, None] * inv, col[:, None] * inv], axis=-1)
    return jnp.cos(ang), jnp.sin(ang)


def apply_rope(x, cos, sin):
    x1, x2 = jnp.split(x.astype(F32), 2, axis=-1)
    c, s = cos[:, None, :], sin[:, None, :]
    return jnp.concatenate([x1 * c - x2 * s, x1 * s + x2 * c], axis=-1).astype(x.dtype)


def softmax_with_sink(logits, sink):
    full = jnp.concatenate([logits, jnp.broadcast_to(sink, logits.shape[:-1] + (1,))], axis=-1)
    return jax.nn.softmax(full, axis=-1)[..., :-1]


def latent_window_attention(q, k, v, k_ctx, v_ctx, sink):
    B, S, H, hd = q.shape
    G = H // A_KV_HEADS
    nb = S // A_BLOCK
    scale = hd ** -0.5
    qb = q.reshape(B, nb, A_BLOCK, A_KV_HEADS, G, hd)

    def band(a):
        ap = jnp.pad(a, ((0, 0), (A_BLOCK, A_BLOCK), (0, 0), (0, 0)))
        ap = ap.reshape(B, nb + 2, A_BLOCK, A_KV_HEADS, hd)
        return jnp.concatenate([ap[:, :-2], ap[:, 1:-1], ap[:, 2:]], axis=2)

    kb, vb = band(k), band(v)
    s_loc = jnp.einsum('bnqhgd,bnkhd->bnhgqk', qb, kb).astype(F32) * scale
    s_ctx = jnp.einsum('bnqhgd,bchd->bnhgqc', qb, k_ctx).astype(F32) * scale
    qpos = jnp.arange(S).reshape(nb, A_BLOCK)
    kpos = (jnp.arange(nb)[:, None] - 1) * A_BLOCK + jnp.arange(3 * A_BLOCK)[None, :]
    valid = ((jnp.abs(qpos[:, :, None] - kpos[:, None, :]) <= WINDOW)
             & (kpos[:, None, :] >= 0) & (kpos[:, None, :] < S))
    s_loc = jnp.where(valid[None, :, None, None], s_loc, -jnp.inf)
    sink_b = sink.astype(F32).reshape(1, 1, A_KV_HEADS, G, 1, 1)
    p = softmax_with_sink(jnp.concatenate([s_loc, s_ctx], axis=-1), sink_b)
    p_loc = p[..., :3 * A_BLOCK].astype(v.dtype)
    p_ctx = p[..., 3 * A_BLOCK:].astype(v.dtype)
    o = (jnp.einsum('bnhgqk,bnkhd->bnqhgd', p_loc, vb)
         + jnp.einsum('bnhgqc,bchd->bnqhgd', p_ctx, v_ctx))
    return o.reshape(B, S, H * hd)


def context_attention(q, k, v, sink):
    B, L, H, hd = q.shape
    G = H // A_KV_HEADS
    qg = q.reshape(B, L, A_KV_HEADS, G, hd)
    s = jnp.einsum('bqhgd,bkhd->bhgqk', qg, k).astype(F32) * hd ** -0.5
    p = softmax_with_sink(s, sink.astype(F32).reshape(1, A_KV_HEADS, G, 1, 1)).astype(v.dtype)
    return jnp.einsum('bhgqk,bkhd->bqhgd', p, v).reshape(B, L, H * hd)


def to_chunks(a):
    B, T, H, d = a.shape
    return a.reshape(B, T // CHUNK, CHUNK, H, d).transpose(1, 0, 3, 2, 4)


def from_chunks(a):
    n, B, H, L, d = a.shape
    return a.transpose(1, 0, 3, 2, 4).reshape(B, n * L, H, d)


def gla_scan(q, k, v, log_a, state):
    tri = jnp.tril(jnp.ones((CHUNK, CHUNK), dtype=bool))

    def step(S, inp):
        qc, kc, vc, ac = inp
        b = jnp.cumsum(ac, axis=2)
        diff = b[:, :, :, None, :] - b[:, :, None, :, :]
        dec = jnp.where(tri[:, :, None], jnp.exp(jnp.where(tri[:, :, None], diff, 0.0)), 0.0)
        att = jnp.einsum('bhid,bhjd,bhijd->bhij', qc, kc, dec)
        o = (jnp.einsum('bhid,bhde->bhie', qc * jnp.exp(b), S)
             + jnp.einsum('bhij,bhje->bhie', att, vc))
        b_last = b[:, :, -1:, :]
        S = (jnp.exp(b_last[:, :, 0, :])[..., None] * S
             + jnp.einsum('bhjd,bhje->bhde', kc * jnp.exp(b_last - b), vc))
        return S, o

    xs = tuple(to_chunks(a.astype(F32)) for a in (q, k, v, log_a))
    S, o = lax.scan(step, state, xs)
    return S, from_chunks(o)


def mlstm_scan(q, k, v, i_pre, log_f, carry):
    tri = jnp.tril(jnp.ones((CHUNK, CHUNK), dtype=bool))

    def step(carry, inp):
        C, n, m = carry
        qc, kc, vc, ic, fc = inp
        ic, fc = ic[..., 0], fc[..., 0]
        b = jnp.cumsum(fc, axis=-1)
        log_w = jnp.where(tri, b[..., :, None] - b[..., None, :] + ic[..., None, :], -jnp.inf)
        log_inter = b + m[..., None]
        m_i = jnp.maximum(log_inter, jnp.max(log_w, axis=-1))
        w = jnp.exp(log_w - m_i[..., None])
        w_inter = jnp.exp(log_inter - m_i)
        s = jnp.einsum('bhid,bhjd->bhij', qc, kc) * w
        num = (w_inter[..., None] * jnp.einsum('bhid,bhde->bhie', qc, C)
               + jnp.einsum('bhij,bhje->bhie', s, vc))
        den = w_inter * jnp.einsum('bhid,bhd->bhi', qc, n) + jnp.sum(s, axis=-1)
        h = num / jnp.maximum(jnp.abs(den), jnp.exp(-m_i))[..., None]
        m_new = m_i[..., -1]
        w_state = jnp.exp(b[..., -1:] - b + ic - m_new[..., None])
        decay = jnp.exp(b[..., -1] + m - m_new)
        C = decay[..., None, None] * C + jnp.einsum('bhj,bhjd,bhje->bhde', w_state, kc, vc)
        n = decay[..., None] * n + jnp.einsum('bhj,bhjd->bhd', w_state, kc)
        return (C, n, m_new), h

    xs = tuple(to_chunks(a.astype(F32)) for a in (q, k, v, i_pre, log_f))
    carry, h = lax.scan(step, carry, xs)
    return carry, from_chunks(h)


def flip_time(a, rev):
    return a[:, ::-1] if rev else a


def bidirectional(scan_fn, init, ctx_dirs, lat_dirs):
    out_ctx, out_lat = [], []
    for d in range(2):
        rev = d == 1
        state, o_c = scan_fn(*(flip_time(a, rev) for a in ctx_dirs[d]), init)
        _, o_l = scan_fn(*(flip_time(a, rev) for a in lat_dirs[d]), state)
        out_ctx.append(flip_time(o_c, rev))
        out_lat.append(flip_time(o_l, rev))
    return out_ctx[0] + out_ctx[1], out_lat[0] + out_lat[1]


def stream_features(h, lp):
    B, T, _ = h.shape
    (aq, ak, av, gq, gk, gv, gr, gg, mq, mk, mv, mo, mi, mf, s_a, s_g, s_m) = split_cols(h @ lp['w_in'])
    f = {}
    f['aq'] = rmsnorm(aq.reshape(B, T, A_HEADS, A_HEAD_DIM), lp['attn_q_norm'])
    f['ak'] = rmsnorm(ak.reshape(B, T, A_KV_HEADS, A_HEAD_DIM), lp['attn_k_norm'])
    f['av'] = av.reshape(B, T, A_KV_HEADS, A_HEAD_DIM)
    f['gq'] = gq.reshape(B, T, G_HEADS, G_DK) * (G_DK ** -0.5)
    f['gk'] = gk.reshape(B, T, G_HEADS, G_DK)
    f['gv'] = gv.reshape(B, T, G_HEADS, G_DV)
    f['gr'] = gr
    gate_lr = gg.reshape(B, T, 2, G_RANK)
    log_a = jax.nn.log_sigmoid(
        (jnp.einsum('btnr,nrk->btnk', gate_lr, lp['gla_w2']) + lp['gla_b']).astype(F32)) / G_TAU
    f['ga'] = tuple(log_a[:, :, d].reshape(B, T, G_HEADS, G_DK) for d in range(2))
    mqk = jax.nn.silu(dwconv_centred(jnp.concatenate([mq, mk], axis=-1),
                                     lp['mlstm_conv_w'], lp['mlstm_conv_b']))
    mq, mk = jnp.split(mqk, 2, axis=-1)
    f['mq'] = mq.reshape(B, T, M_HEADS, M_HEAD_DIM)
    f['mk'] = mk.reshape(B, T, M_HEADS, M_HEAD_DIM) * (M_HEAD_DIM ** -0.5)
    f['mv'] = mv.reshape(B, T, M_HEADS, M_HEAD_DIM)
    f['mo'] = mo
    i_pre = (mi.reshape(B, T, 2, M_HEADS) + lp['mlstm_ib']).astype(F32)
    log_f = jax.nn.log_sigmoid((mf.reshape(B, T, 2, M_HEADS) + lp['mlstm_fb']).astype(F32))
    f['mi'] = tuple(i_pre[:, :, d, :, None] for d in range(2))
    f['mf'] = tuple(log_f[:, :, d, :, None] for d in range(2))
    f['branch_gates'] = (jax.nn.sigmoid(s_a), jax.nn.sigmoid(s_g), jax.nn.sigmoid(s_m))
    return f


def merge_branches(f, att, gla, mlstm, lp):
    dt = att.dtype
    B, T = att.shape[:2]
    g = rmsnorm(gla.astype(dt), lp['gla_norm']).reshape(B, T, G_V) * jax.nn.silu(f['gr'])
    m = rmsnorm(mlstm.astype(dt), lp['mlstm_norm']).reshape(B, T, M_W) * jax.nn.sigmoid(f['mo'])
    ga, gg, gm = f['branch_gates']
    y = (ga * (att @ lp['w_out_attn']) + gg * (g @ lp['w_out_gla'])
         + gm * (m @ lp['w_out_mlstm']))
    return y @ lp['w_o']


def token_mixer(xn, cn, cos, sin, lp, want_ctx):
    B = xn.shape[0]
    fl = stream_features(xn, lp)
    fc = stream_features(cn, lp)
    q_lat = apply_rope(fl['aq'], cos, sin)
    k_lat = apply_rope(fl['ak'], cos, sin)
    att_l = latent_window_attention(q_lat, k_lat, fl['av'], fc['ak'], fc['av'], lp['attn_sink'])
    g0 = jnp.zeros((B, G_HEADS, G_DK, G_DV), F32)
    gla_c, gla_l = bidirectional(
        gla_scan, g0,
        tuple((fc['gq'], fc['gk'], fc['gv'], fc['ga'][d]) for d in range(2)),
        tuple((fl['gq'], fl['gk'], fl['gv'], fl['ga'][d]) for d in range(2)))
    m0 = (jnp.zeros((B, M_HEADS, M_HEAD_DIM, M_HEAD_DIM), F32),
          jnp.zeros((B, M_HEADS, M_HEAD_DIM), F32),
          jnp.zeros((B, M_HEADS), F32))
    ml_c, ml_l = bidirectional(
        mlstm_scan, m0,
        tuple((fc['mq'], fc['mk'], fc['mv'], fc['mi'][d], fc['mf'][d]) for d in range(2)),
        tuple((fl['mq'], fl['mk'], fl['mv'], fl['mi'][d], fl['mf'][d]) for d in range(2)))
    y_lat = merge_branches(fl, att_l, gla_l, ml_l, lp)
    if not want_ctx:
        return y_lat, None
    att_c = context_attention(fc['aq'], fc['ak'], fc['av'], lp['attn_sink'])
    y_ctx = merge_branches(fc, att_c, gla_c, ml_c, lp)
    return y_lat, y_ctx


def adaln(h, m, j, gain):
    return rmsnorm(h, gain) * (1 + m[..., 3 * j + 1, :, :]) + m[..., 3 * j, :, :]


def trunk_layer(x, ctx, c, c_ctx, cos, sin, lp, last):
    B = x.shape[0]
    mod_l = (jax.nn.silu(c) @ lp['mod_w'] + lp['mod_b']).reshape(B, N_MOD, 1, D_MODEL)
    mod_c = (jax.nn.silu(c_ctx) @ lp['mod_w'] + lp['mod_b']).reshape(N_MOD, 1, D_MODEL)
    g = lp['norm_g']
    x = x + 0.5 * mod_l[:, 2] * swiglu(adaln(x, mod_l, 0, g[0]), lp['ffn1_w13'], lp['ffn1_w2'])
    ctx = ctx + 0.5 * mod_c[2] * swiglu(adaln(ctx, mod_c, 0, g[0]), lp['ffn1_w13'], lp['ffn1_w2'])
    y_l, y_c = token_mixer(adaln(x, mod_l, 1, g[1]), adaln(ctx, mod_c, 1, g[1]), cos, sin, lp, not last)
    x = x + mod_l[:, 5] * y_l
    x = x + 0.5 * mod_l[:, 8] * swiglu(adaln(x, mod_l, 2, g[2]), lp['ffn2_w13'], lp['ffn2_w2'])
    if not last:
        ctx = ctx + mod_c[5] * y_c
        ctx = ctx + 0.5 * mod_c[8] * swiglu(adaln(ctx, mod_c, 2, g[2]), lp['ffn2_w13'], lp['ffn2_w2'])
    return x, ctx


def setup_inputs(seed: int = 0) -> dict:
    key = jax.random.key(seed)
    ks = iter(jax.random.split(key, 40))

    def nrm(shape, s):
        return jax.random.normal(next(ks), shape, F32) * s

    def uni(shape):
        return jax.random.uniform(next(ks), shape, F32)

    D = D_MODEL
    return {
        'x': nrm((BATCH, SEQ, D), 1.0),
        'c': nrm((BATCH, D), 1.0),
        'ctx': nrm((BATCH, CTX_LEN, D), 1.0),
        'c_ctx': nrm((D,), 1.0),
        'mod_w': nrm((DEPTH, D, N_MOD * D), D ** -0.5),
        'mod_b': nrm((DEPTH, N_MOD * D), 0.02),
        'norm_g': 1.0 + nrm((DEPTH, 3, D), 0.02),
        'ffn1_w13': nrm((DEPTH, D, 2 * D_FF), D ** -0.5),
        'ffn1_w2': nrm((DEPTH, D_FF, D), D_FF ** -0.5),
        'ffn2_w13': nrm((DEPTH, D, 2 * D_FF), D ** -0.5),
        'ffn2_w2': nrm((DEPTH, D_FF, D), D_FF ** -0.5),
        'w_in': nrm((DEPTH, D, D_IN), D ** -0.5),
        'attn_q_norm': 1.0 + nrm((DEPTH, A_HEAD_DIM), 0.02),
        'attn_k_norm': 1.0 + nrm((DEPTH, A_HEAD_DIM), 0.02),
        'attn_sink': nrm((DEPTH, A_HEADS), 0.5),
        'gla_w2': nrm((DEPTH, 2, G_RANK, G_QK), G_RANK ** -0.5),
        'gla_b': 1.0 + nrm((DEPTH, 2, G_QK), 0.1),
        'gla_norm': 1.0 + nrm((DEPTH, G_DV), 0.02),
        'mlstm_conv_w': nrm((DEPTH, M_CONV, 2 * M_W), M_CONV ** -0.5),
        'mlstm_conv_b': nrm((DEPTH, 2 * M_W), 0.02),
        'mlstm_ib': nrm((DEPTH, 2, M_HEADS), 0.1),
        'mlstm_fb': 3.0 + 3.0 * uni((DEPTH, 2, M_HEADS)),
        'mlstm_norm': 1.0 + nrm((DEPTH, M_HEAD_DIM), 0.02),
        'w_out_attn': nrm((DEPTH, A_Q, D), A_Q ** -0.5),
        'w_out_gla': nrm((DEPTH, G_V, D), G_V ** -0.5),
        'w_out_mlstm': nrm((DEPTH, M_W, D), M_W ** -0.5),
        'w_o': nrm((DEPTH, D, D), D ** -0.5),
    }


def reference(x, c, ctx, c_ctx, mod_w, mod_b, norm_g, ffn1_w13, ffn1_w2, ffn2_w13, ffn2_w2,
              w_in, attn_q_norm, attn_k_norm, attn_sink, gla_w2, gla_b, gla_norm,
              mlstm_conv_w, mlstm_conv_b, mlstm_ib, mlstm_fb, mlstm_norm,
              w_out_attn, w_out_gla, w_out_mlstm, w_o):
    rows = x.shape[1] // GRID_W
    cos, sin = axial_rope(rows)
    for l in range(DEPTH):
        lp = {
            'mod_w': mod_w[l], 'mod_b': mod_b[l], 'norm_g': norm_g[l],
            'ffn1_w13': ffn1_w13[l], 'ffn1_w2': ffn1_w2[l],
            'ffn2_w13': ffn2_w13[l], 'ffn2_w2': ffn2_w2[l],
            'w_in': w_in[l], 'attn_q_norm': attn_q_norm[l], 'attn_k_norm': attn_k_norm[l],
            'attn_sink': attn_sink[l], 'gla_w2': gla_w2[l], 'gla_b': gla_b[l],
            'gla_norm': gla_norm[l], 'mlstm_conv_w': mlstm_conv_w[l],
            'mlstm_conv_b': mlstm_conv_b[l], 'mlstm_ib': mlstm_ib[l], 'mlstm_fb': mlstm_fb[l],
            'mlstm_norm': mlstm_norm[l], 'w_out_attn': w_out_attn[l], 'w_out_gla': w_out_gla[l],
            'w_out_mlstm': w_out_mlstm[l], 'w_o': w_o[l],
        }
        x, ctx = trunk_layer(x, ctx, c, c_ctx, cos, sin, lp, l == DEPTH - 1)
    return x
```

```python
import functools

import jax
import jax.numpy as jnp
import numpy as np
from jax import lax
from jax.experimental import pallas as pl
from jax.experimental.pallas import tpu as pltpu

F32 = jnp.float32
BF16 = jnp.bfloat16

D_MODEL = 1024
GRID_W = 64
A_HEADS, A_KV_HEADS, A_HEAD_DIM = 8, 2, 64
WINDOW = A_BLOCK = 128
ROPE_BASE = 10000.0
G_HEADS, G_DK, G_DV, G_RANK, G_TAU = 4, 64, 128, 16, 16.0
M_HEADS, M_HEAD_DIM, M_CONV = 4, 128, 5
D_FF = 2816
N_MOD = 9
EPS = 1e-6
A_Q = A_HEADS * A_HEAD_DIM
A_KV = A_KV_HEADS * A_HEAD_DIM
G_QK = G_HEADS * G_DK
G_V = G_HEADS * G_DV
M_W = M_HEADS * M_HEAD_DIM
IN_SPLITS = (A_Q, A_KV, A_KV, G_QK, G_QK, G_V, G_V, 2 * G_RANK,
             M_W, M_W, M_W, M_W, 2 * M_HEADS, 2 * M_HEADS, D_MODEL, D_MODEL, D_MODEL)

LANES = 128
BF16_SUBLANES = 16
VMEM_LIMIT_BYTES = 56 * 1024 * 1024

FFN_TM = 768
INPROJ_TM = 384
FFN_FK = 256
GLA_CHUNK = 128
MLSTM_CHUNK = 256
MOD_ROWS = 8

SMALL_GG = 0
SMALL_MI = 2 * G_RANK
SMALL_MF = SMALL_MI + 2 * M_HEADS

_OFF = {}
_o = 0
for _name, _w in (("aq", A_Q), ("akv", 2 * A_KV), ("gqk", 2 * G_QK), ("gv", G_V), ("gr", G_V),
                  ("mqk", 2 * M_W), ("mv", M_W), ("mo", M_W), ("gates", 3 * D_MODEL), ("small", LANES)):
    _OFF[_name] = (_o, _o + _w)
    _o += _w
D_IN_PAD = _o


def _dot(a, b):
    return jnp.dot(a, b, preferred_element_type=F32)


def _dot_nt(a, b):
    return lax.dot_general(a, b, (((1,), (1,)), ((), ())), preferred_element_type=F32)


def _dot_tn(a, b):
    return lax.dot_general(a, b, (((0,), (0,)), ((), ())), preferred_element_type=F32)


def _cumdot(tri, x):
    hi = x.astype(BF16)
    r1 = x - hi.astype(F32)
    mid = r1.astype(BF16)
    lo = (r1 - mid.astype(F32)).astype(BF16)
    return _dot(tri, hi) + _dot(tri, mid) + _dot(tri, lo)


def _silu(x):
    return x * jax.nn.sigmoid(x)


def _log_sigmoid(x):
    return jnp.minimum(x, 0.0) - jnp.log1p(jnp.exp(-jnp.abs(x)))


def _rmsnorm(x, gain):
    return x * lax.rsqrt(jnp.mean(x * x, axis=-1, keepdims=True) + EPS) * gain


def _mod_rows(mod_ref, j, b, t, tm, ctx_len):
    row = t * tm + lax.broadcasted_iota(jnp.int32, (tm, 1), 0)
    is_ctx = row < ctx_len
    n_b = mod_ref.shape[1] - 1
    out = []
    for q in range(3):
        lat = mod_ref[3 * j + q, pl.ds(b, 1), :]
        ctx = mod_ref[3 * j + q, n_b:n_b + 1, :]
        out.append(jnp.where(is_ctx, ctx, lat))
    return out


def _mod_kernel(c_ref, w_ref, b_ref, o_ref):
    h = _silu(c_ref[...]).astype(BF16)
    o_ref[...] = _dot(h, w_ref[...].astype(BF16)) + b_ref[...]


def _modulation(cc, mod_w, mod_b):
    depth, d, _ = mod_w.shape
    rows = cc.shape[0]
    return pl.pallas_call(
        _mod_kernel,
        grid=(depth, N_MOD),
        in_specs=[
            pl.BlockSpec((rows, d), lambda l, j: (0, 0)),
            pl.BlockSpec((None, d, d), lambda l, j: (l, 0, j)),
            pl.BlockSpec((None, None, 1, d), lambda l, j: (l, j, 0, 0)),
        ],
        out_specs=pl.BlockSpec((None, None, rows, d), lambda l, j: (l, j, 0, 0)),
        out_shape=jax.ShapeDtypeStruct((depth, N_MOD, rows, d), F32),
        compiler_params=pltpu.CompilerParams(dimension_semantics=("arbitrary", "arbitrary")),
        name="modulation",
    )(cc, mod_w, mod_b.reshape(depth, N_MOD, 1, d))


def _ffn_kernel(x_ref, mod_ref, g_ref, w13_ref, w2_ref, o_ref, h_ref, acc_ref, *, j, tm, ctx_len):
    b, t = pl.program_id(0), pl.program_id(1)
    x = x_ref[...]
    shift, scale, gate = _mod_rows(mod_ref, j, b, t, tm, ctx_len)
    h_ref[...] = (_rmsnorm(x, g_ref[...]) * (1.0 + scale) + shift).astype(BF16)
    for k in range(D_FF // FFN_FK):
        lo, hi = k * FFN_FK, (k + 1) * FFN_FK
        a = _dot(h_ref[...], w13_ref[:, lo:hi])
        g = _dot(h_ref[...], w13_ref[:, D_FF + lo:D_FF + hi])
        contrib = _dot((_silu(a) * g).astype(BF16), w2_ref[lo:hi, :])
        if k == 0:
            acc_ref[...] = contrib
        else:
            acc_ref[...] += contrib
    o_ref[...] = x + (0.5 * gate) * acc_ref[...]


def _resident(shape):
    return pl.BlockSpec(shape, lambda *_: (0,) * len(shape), pipeline_mode=pl.Buffered(1))


def _ffn(xs, mod, gain, w13, w2, *, j, n_batch, rows_per_batch, ctx_len):
    tm = FFN_TM
    tpb = rows_per_batch // tm
    d = xs.shape[1]
    tok = pl.BlockSpec((tm, d), lambda b, t: (b * tpb + t, 0))
    return pl.pallas_call(
        functools.partial(_ffn_kernel, j=j, tm=tm, ctx_len=ctx_len),
        grid=(n_batch, tpb),
        in_specs=[tok, _resident(mod.shape), _resident(gain.shape), _resident(w13.shape), _resident(w2.shape)],
        out_specs=tok,
        out_shape=jax.ShapeDtypeStruct(xs.shape, F32),
        scratch_shapes=[pltpu.VMEM((tm, d), BF16), pltpu.VMEM((tm, d), F32)],
        compiler_params=pltpu.CompilerParams(dimension_semantics=("parallel", "parallel"),
                                             vmem_limit_bytes=VMEM_LIMIT_BYTES),
        name=f"ffn{j}",
    )(xs, mod, gain, w13, w2)


def _headnorm_rope(z, gain, cos, sin):
    lane = lax.broadcasted_iota(jnp.int32, z.shape, 1)
    lo = lane < A_HEAD_DIM
    sq = z * z
    s_lo = jnp.sum(jnp.where(lo, sq, 0.0), axis=-1, keepdims=True)
    s_hi = jnp.sum(jnp.where(lo, 0.0, sq), axis=-1, keepdims=True)
    inv = lax.rsqrt(jnp.where(lo, s_lo, s_hi) * (1.0 / A_HEAD_DIM) + EPS)
    y = z * inv * gain
    first_half = (lane & (A_HEAD_DIM // 2)) == 0
    partner = jnp.where(first_half, pltpu.roll(y, LANES - A_HEAD_DIM // 2, 1), pltpu.roll(y, A_HEAD_DIM // 2, 1))
    return y * cos + partner * sin


def _spread_kv(z):
    lane = lax.broadcasted_iota(jnp.int32, z.shape, 1)
    lo = lane < A_HEAD_DIM
    a0 = jnp.where(lo, z, 0.0)
    b1 = jnp.where(lo, 0.0, z)
    return [a0, pltpu.roll(a0, A_HEAD_DIM, 1), pltpu.roll(b1, A_HEAD_DIM, 1), b1]


def _inproj_kernel(x_ref, mod_ref, g_ref, w_ref, wg_ref, bg_ref, qn_ref, kn_ref, cos_ref, sin_ref, sb_ref,
                   q_o, k4_o, v4_o, gq_o, gk_o, gv_o, gr_o, mqk_o, mv_o, mo_o, gates_o, small_o, la_o,
                   h_ref, *, tm, ctx_len):
    b, t = pl.program_id(0), pl.program_id(1)
    shift, scale, _ = _mod_rows(mod_ref, 1, b, t, tm, ctx_len)
    h_ref[...] = (_rmsnorm(x_ref[...], g_ref[...]) * (1.0 + scale) + shift).astype(BF16)

    def proj(name, lo, hi):
        base = _OFF[name][0]
        return _dot(h_ref[...], w_ref[:, base + lo:base + hi])

    cos, sin = cos_ref[...], sin_ref[...]
    z = proj("aq", 0, A_Q)
    for s in range(A_Q // LANES):
        y = _headnorm_rope(z[:, s * LANES:(s + 1) * LANES], qn_ref[...], cos, sin)
        q_o[:, s * LANES:(s + 1) * LANES] = (y * (A_HEAD_DIM ** -0.5)).astype(BF16)
    z = proj("akv", 0, 2 * A_KV)
    for s, part in enumerate(_spread_kv(_headnorm_rope(z[:, :A_KV], kn_ref[...], cos, sin))):
        k4_o[:, s * LANES:(s + 1) * LANES] = part.astype(BF16)
    for s, part in enumerate(_spread_kv(z[:, A_KV:])):
        v4_o[:, s * LANES:(s + 1) * LANES] = part.astype(BF16)
    z = proj("gqk", 0, 2 * G_QK)
    gq_o[...] = (z[:, :G_QK] * (G_DK ** -0.5)).astype(BF16)
    gk_o[...] = z[:, G_QK:].astype(BF16)
    gv_o[...] = proj("gv", 0, G_V).astype(BF16)
    gr_o[...] = _silu(proj("gr", 0, G_V)).astype(BF16)
    mqk_o[:, :M_W] = proj("mqk", 0, M_W).astype(BF16)
    mqk_o[:, M_W:] = proj("mqk", M_W, 2 * M_W).astype(BF16)
    mv_o[...] = proj("mv", 0, M_W).astype(BF16)
    mo_o[...] = jax.nn.sigmoid(proj("mo", 0, M_W)).astype(BF16)
    for s in range(3 * D_MODEL // 512):
        gates_o[:, s * 512:(s + 1) * 512] = jax.nn.sigmoid(proj("gates", s * 512, (s + 1) * 512)).astype(BF16)
    zs = proj("small", 0, LANES)
    lane = lax.broadcasted_iota(jnp.int32, zs.shape, 1)
    zb = zs + sb_ref[...]
    is_f = (lane >= SMALL_MF) & (lane < SMALL_MF + 2 * M_HEADS)
    small_o[...] = jnp.where(is_f, _log_sigmoid(zb), zb)
    la_o[...] = _log_sigmoid(_dot(zs.astype(BF16), wg_ref[...]) + bg_ref[...]) * (1.0 / G_TAU)


def _inproj(xs, mod, gain, w_in, wg, bg, qn, kn, cos, sin, sb, *, n_batch, rows_per_batch, ctx_len):
    tm = INPROJ_TM
    tpb = rows_per_batch // tm
    n_tok, d = xs.shape

    def tok(width):
        return pl.BlockSpec((tm, width), lambda b, t: (b * tpb + t, 0))

    pos = pl.BlockSpec((tm, LANES), lambda b, t: (t, 0))
    widths = (A_Q, 4 * LANES, 4 * LANES, G_QK, G_QK, G_V, G_V, 2 * M_W, M_W, M_W, 3 * D_MODEL)
    out_shape = [jax.ShapeDtypeStruct((n_tok, w), BF16) for w in widths]
    out_shape += [jax.ShapeDtypeStruct((n_tok, LANES), F32), jax.ShapeDtypeStruct((n_tok, 2 * G_QK), F32)]
    out_specs = [tok(w) for w in widths] + [tok(LANES), tok(2 * G_QK)]
    return pl.pallas_call(
        functools.partial(_inproj_kernel, tm=tm, ctx_len=ctx_len),
        grid=(n_batch, tpb),
        in_specs=[tok(d), _resident(mod.shape), _resident(gain.shape), _resident(w_in.shape),
                  _resident(wg.shape), _resident(bg.shape), _resident(qn.shape), _resident(kn.shape),
                  pos, pos, _resident(sb.shape)],
        out_specs=out_specs,
        out_shape=out_shape,
        scratch_shapes=[pltpu.VMEM((tm, d), BF16)],
        compiler_params=pltpu.CompilerParams(dimension_semantics=("parallel", "parallel"),
                                             vmem_limit_bytes=VMEM_LIMIT_BYTES),
        name="inproj",
    )(xs, mod, gain, w_in, wg, bg, qn, kn, cos, sin, sb)


def _mconv_kernel(prev_ref, cur_ref, next_ref, w_ref, b_ref, q_o, k_o, e_ref, *, tm, rows_per_batch, ctx_len):
    t = pl.program_id(1)
    halo = BF16_SUBLANES
    e_ref[0:halo, :] = prev_ref[...].astype(F32)
    e_ref[halo:halo + tm, :] = cur_ref[...].astype(F32)
    e_ref[halo + tm:, :] = next_ref[...].astype(F32)
    row = t * tm + lax.broadcasted_iota(jnp.int32, (tm, 1), 0)
    pad = M_CONV // 2
    acc = None
    for tap in range(M_CONV):
        off = tap - pad
        nb = row + off
        valid = (nb >= 0) & (nb < rows_per_batch) & ((row < ctx_len) == (nb < ctx_len))
        term = jnp.where(valid, e_ref[pl.ds(halo + off, tm), :], 0.0) * w_ref[tap:tap + 1, :]
        acc = term if acc is None else acc + term
    y = _silu(acc + b_ref[...])
    q_o[...] = y[:, :M_W].astype(BF16)
    k_o[...] = (y[:, M_W:] * (M_HEAD_DIM ** -0.5)).astype(BF16)


def _mconv(mqk, w, bias, *, n_batch, rows_per_batch, ctx_len):
    tm = FFN_TM
    tpb = rows_per_batch // tm
    n_tok, width = mqk.shape
    hb = tm // BF16_SUBLANES
    n_hb = n_tok // BF16_SUBLANES
    cur = pl.BlockSpec((tm, width), lambda b, t: (b * tpb + t, 0))
    prev = pl.BlockSpec((BF16_SUBLANES, width), lambda b, t: (jnp.maximum((b * tpb + t) * hb - 1, 0), 0))
    nxt = pl.BlockSpec((BF16_SUBLANES, width), lambda b, t: (jnp.minimum((b * tpb + t + 1) * hb, n_hb - 1), 0))
    half = pl.BlockSpec((tm, M_W), lambda b, t: (b * tpb + t, 0))
    return pl.pallas_call(
        functools.partial(_mconv_kernel, tm=tm, rows_per_batch=rows_per_batch, ctx_len=ctx_len),
        grid=(n_batch, tpb),
        in_specs=[prev, cur, nxt, _resident(w.shape), _resident(bias.shape)],
        out_specs=[half, half],
        out_shape=[jax.ShapeDtypeStruct((n_tok, M_W), BF16)] * 2,
        scratch_shapes=[pltpu.VMEM((tm + 2 * BF16_SUBLANES, width), F32)],
        compiler_params=pltpu.CompilerParams(dimension_semantics=("parallel", "parallel"),
                                             vmem_limit_bytes=VMEM_LIMIT_BYTES),
        name="mconv",
    )(mqk, mqk, mqk, w, bias)


def _attn_kernel(sink_ref, q_ref, kp_ref, kc_ref, kn_ref, kx_ref, vp_ref, vc_ref, vn_ref, vx_ref, o_ref,
                 *, n_ctx_blk, n_blk):
    j = pl.program_id(1)
    blk = A_BLOCK
    off = jnp.int32(4 * blk)
    cur_off = jnp.where(j >= n_ctx_blk, 0, off)
    prev_off = jnp.where(j >= n_ctx_blk + 1, 0, off)
    next_off = jnp.where((j >= n_ctx_blk) & (j <= n_blk - 2), 0, off)
    r = lax.broadcasted_iota(jnp.int32, (blk, blk), 0)
    c = lax.broadcasted_iota(jnp.int32, (blk, blk), 1)
    neg = -jnp.inf
    n_ctx = kx_ref.shape[0]
    bias = jnp.concatenate([
        jnp.where(c >= r + prev_off, 0.0, neg),
        jnp.where(c >= cur_off, 0.0, neg),
        jnp.where(c <= r - next_off, 0.0, neg),
        jnp.zeros((blk, n_ctx), F32)], axis=1)
    bias = jnp.concatenate([bias, bias], axis=0)
    top = lax.broadcasted_iota(jnp.int32, (2 * blk, 1), 0) < blk
    k_all = jnp.concatenate([kp_ref[...], kc_ref[...], kn_ref[...], kx_ref[...]], axis=0)
    v_all = jnp.concatenate([vp_ref[...], vc_ref[...], vn_ref[...], vx_ref[...]], axis=0)
    heads_per_kv = A_HEADS // A_KV_HEADS
    for g in range(A_KV_HEADS):
        c0 = g * 2 * LANES
        qg = jnp.concatenate([q_ref[:, c0:c0 + LANES], q_ref[:, c0 + LANES:c0 + 2 * LANES]], axis=0)
        acc = None
        for half in range(2):
            col = (2 * g + half) * LANES
            s = _dot_nt(qg, k_all[:, col:col + LANES]) + bias
            sink = jnp.where(top, sink_ref[heads_per_kv * g + half], sink_ref[heads_per_kv * g + 2 + half])
            m = jnp.maximum(jnp.max(s, axis=-1, keepdims=True), sink)
            p = jnp.exp(s - m)
            denom = jnp.sum(p, axis=-1, keepdims=True) + jnp.exp(sink - m)
            o = _dot(p.astype(BF16), v_all[:, col:col + LANES]) * (1.0 / denom)
            acc = o if acc is None else acc + o
        o_ref[:, c0:c0 + LANES] = acc[:blk].astype(BF16)
        o_ref[:, c0 + LANES:c0 + 2 * LANES] = acc[blk:].astype(BF16)


def _attention(sink, q, k4, v4, *, n_batch, rows_per_batch, ctx_len):
    blk = A_BLOCK
    n_blk = rows_per_batch // blk
    n_ctx_blk = ctx_len // blk
    n_tok, width = q.shape
    cur = lambda b, j: (b * n_blk + j, 0)
    prev = lambda b, j: (b * n_blk + jnp.maximum(j - 1, 0), 0)
    nxt = lambda b, j: (b * n_blk + jnp.minimum(j + 1, n_blk - 1), 0)
    ctx = lambda b, j: (b * (rows_per_batch // ctx_len), 0)
    kv_specs = [pl.BlockSpec((blk, width), prev), pl.BlockSpec((blk, width), cur),
                pl.BlockSpec((blk, width), nxt), pl.BlockSpec((ctx_len, width), ctx)]
    return pl.pallas_call(
        functools.partial(_attn_kernel, n_ctx_blk=n_ctx_blk, n_blk=n_blk),
        grid=(n_batch, n_blk),
        in_specs=[pl.BlockSpec(memory_space=pltpu.SMEM), pl.BlockSpec((blk, width), cur)] + kv_specs + kv_specs,
        out_specs=pl.BlockSpec((blk, width), cur),
        out_shape=jax.ShapeDtypeStruct((n_tok, width), BF16),
        compiler_params=pltpu.CompilerParams(dimension_semantics=("parallel", "parallel"),
                                             vmem_limit_bytes=VMEM_LIMIT_BYTES),
        name="attention",
    )(sink, q, k4, k4, k4, k4, v4, v4, v4, v4)


def _gla_direction(q_ref, k_ref, v_ref, la_ref, o_ref, st_ref, *, reverse):
    n = q_ref.shape[0]
    r = lax.broadcasted_iota(jnp.int32, (n, n), 0)
    c = lax.broadcasted_iota(jnp.int32, (n, n), 1)
    keep = (c >= r) if reverse else (c <= r)
    last = 0 if reverse else n - 1
    tri = jnp.where(keep, 1.0, 0.0).astype(BF16)
    cum = _cumdot(tri, la_ref[...])
    mid = cum[n // 2:n // 2 + 1, :]
    end = cum[last:last + 1, :]
    qf, kf = q_ref[...].astype(F32), k_ref[...].astype(F32)
    q_in = (qf * jnp.exp(cum - mid)).astype(BF16)
    k_in = (kf * jnp.exp(mid - cum)).astype(BF16)
    q_st = (qf * jnp.exp(cum)).astype(BF16)
    k_st = (kf * jnp.exp(end - cum)).astype(BF16)
    decay = jnp.exp(end)
    lane = lax.broadcasted_iota(jnp.int32, (n, LANES), 1)
    lo = lane < G_DK
    zero = jnp.zeros((), BF16)
    rr = lax.broadcasted_iota(jnp.int32, (2 * G_DV, 2 * G_DK), 0)
    cc = lax.broadcasted_iota(jnp.int32, (2 * G_DV, 2 * G_DK), 1)
    diag = (rr < G_DV) == (cc < G_DK)
    for s in range(G_HEADS // 2):
        sl = slice(s * LANES, (s + 1) * LANES)
        qs, ks = q_in[:, sl], k_in[:, sl]
        a0 = jnp.where(keep, _dot_nt(qs, jnp.where(lo, ks, zero)), 0.0).astype(BF16)
        a1 = jnp.where(keep, _dot_nt(qs, jnp.where(lo, zero, ks)), 0.0).astype(BF16)
        state = st_ref[s]
        inter = _dot_nt(q_st[:, sl], state.astype(BF16))
        v0 = v_ref[:, 2 * s * G_DV:(2 * s + 1) * G_DV]
        v1 = v_ref[:, (2 * s + 1) * G_DV:(2 * s + 2) * G_DV]
        o_ref[:, 2 * s * G_DV:(2 * s + 1) * G_DV] = (_dot(a0, v0) + inter[:, :G_DV]).astype(o_ref.dtype)
        o_ref[:, (2 * s + 1) * G_DV:(2 * s + 2) * G_DV] = (_dot(a1, v1) + inter[:, G_DV:]).astype(o_ref.dtype)
        upd = _dot_tn(v_ref[:, 2 * s * G_DV:(2 * s + 2) * G_DV], k_st[:, sl])
        st_ref[s] = decay[:, sl] * state + jnp.where(diag, upd, 0.0)


def _gla_kernel(qf_ref, kf_ref, vf_ref, laf_ref, qb_ref, kb_ref, vb_ref, lab_ref, of_ref, ob_ref, sf_ref, sb_ref):
    @pl.when(pl.program_id(1) == 0)
    def _():
        sf_ref[...] = jnp.zeros_like(sf_ref)
        sb_ref[...] = jnp.zeros_like(sb_ref)

    _gla_direction(qf_ref, kf_ref, vf_ref, laf_ref, of_ref, sf_ref, reverse=False)
    _gla_direction(qb_ref, kb_ref, vb_ref, lab_ref, ob_ref, sb_ref, reverse=True)


def _scan_maps(n_chunks, n_ctx_chunks):
    def fwd(b, i):
        return b * n_chunks + i

    def bwd(b, i):
        return b * n_chunks + jnp.where(i < n_ctx_chunks, n_ctx_chunks - 1 - i, n_chunks - 1 + n_ctx_chunks - i)

    return fwd, bwd


def _gla(gq, gk, gv, la, *, n_batch, rows_per_batch, ctx_len):
    n = GLA_CHUNK
    n_chunks = rows_per_batch // n
    fwd, bwd = _scan_maps(n_chunks, ctx_len // n)
    n_tok = gq.shape[0]

    def specs(row):
        return [pl.BlockSpec((n, G_QK), lambda b, i: (row(b, i), 0)),
                pl.BlockSpec((n, G_QK), lambda b, i: (row(b, i), 0)),
                pl.BlockSpec((n, G_V), lambda b, i: (row(b, i), 0))]

    state = pltpu.VMEM((G_HEADS // 2, 2 * G_DV, 2 * G_DK), F32)
    return pl.pallas_call(
        _gla_kernel,
        grid=(n_batch, n_chunks),
        in_specs=(specs(fwd) + [pl.BlockSpec((n, G_QK), lambda b, i: (fwd(b, i), 0))]
                  + specs(bwd) + [pl.BlockSpec((n, G_QK), lambda b, i: (bwd(b, i), 1))]),
        out_specs=[pl.BlockSpec((n, G_V), lambda b, i: (fwd(b, i), 0)),
                   pl.BlockSpec((n, G_V), lambda b, i: (bwd(b, i), 0))],
        out_shape=[jax.ShapeDtypeStruct((n_tok, G_V), BF16)] * 2,
        scratch_shapes=[state, state],
        compiler_params=pltpu.CompilerParams(dimension_semantics=("arbitrary", "arbitrary"),
                                             vmem_limit_bytes=VMEM_LIMIT_BYTES),
        name="gla",
    )(gq, gk, gv, la, gq, gk, gv, la)


def _mlstm_direction(q_ref, k_ref, v_ref, small_ref, o_ref, c_ref, n_ref, m_ref, *, direction):
    n = q_ref.shape[0]
    reverse = direction == 1
    r = lax.broadcasted_iota(jnp.int32, (n, n), 0)
    c = lax.broadcasted_iota(jnp.int32, (n, n), 1)
    keep = (c >= r) if reverse else (c <= r)
    last = 0 if reverse else n - 1
    tri = jnp.where(keep, 1.0, 0.0).astype(BF16)
    small = small_ref[...]
    small_t = small.T
    cum = _cumdot(tri, small)
    cum_t = cum.T
    for h in range(M_HEADS):
        idx = direction * M_HEADS + h
        li, lf = SMALL_MI + idx, SMALL_MF + idx
        b_col, b_row = cum[:, lf:lf + 1], cum_t[lf:lf + 1, :]
        i_col, i_row = small[:, li:li + 1], small_t[li:li + 1, :]
        m_prev = m_ref[idx][0:1, 0:1]
        log_w = jnp.where(keep, b_col - b_row + i_row, -jnp.inf)
        log_inter = b_col + m_prev
        m_i = jnp.maximum(log_inter, jnp.max(log_w, axis=-1, keepdims=True))
        w = jnp.exp(log_w - m_i)
        w_inter = jnp.exp(log_inter - m_i)
        sl = slice(h * M_HEAD_DIM, (h + 1) * M_HEAD_DIM)
        qh, kh, vh = q_ref[:, sl], k_ref[:, sl], v_ref[:, sl]
        s = _dot_nt(qh, kh) * w
        c_state, n_state = c_ref[idx], n_ref[idx][0:1, :]
        num = w_inter * _dot(qh, c_state.astype(BF16)) + _dot(s.astype(BF16), vh)
        den = (w_inter * jnp.sum(qh.astype(F32) * n_state, axis=-1, keepdims=True)
               + jnp.sum(s, axis=-1, keepdims=True))
        o_ref[:, sl] = (num / jnp.maximum(jnp.abs(den), jnp.exp(-m_i))).astype(o_ref.dtype)
        m_new = m_i[last:last + 1, :]
        b_end = b_col[last:last + 1, :]
        kw = kh.astype(F32) * jnp.exp(b_end - b_col + i_col - m_new)
        decay = jnp.exp(b_end + m_prev - m_new)
        c_ref[idx] = decay * c_state + _dot_tn(kw.astype(BF16), vh)
        n_new = decay * n_state + jnp.sum(kw, axis=0, keepdims=True)
        n_ref[idx] = jnp.broadcast_to(n_new, n_ref.shape[1:])
        m_ref[idx] = jnp.broadcast_to(m_new, m_ref.shape[1:])


def _mlstm_kernel(qf_ref, kf_ref, vf_ref, sf_ref, qb_ref, kb_ref, vb_ref, sb_ref, of_ref, ob_ref,
                  c_ref, n_ref, m_ref):
    @pl.when(pl.program_id(1) == 0)
    def _():
        c_ref[...] = jnp.zeros_like(c_ref)
        n_ref[...] = jnp.zeros_like(n_ref)
        m_ref[...] = jnp.zeros_like(m_ref)

    _mlstm_direction(qf_ref, kf_ref, vf_ref, sf_ref, of_ref, c_ref, n_ref, m_ref, direction=0)
    _mlstm_direction(qb_ref, kb_ref, vb_ref, sb_ref, ob_ref, c_ref, n_ref, m_ref, direction=1)


def _mlstm(mq, mk, mv, small, *, n_batch, rows_per_batch, ctx_len):
    n = MLSTM_CHUNK
    n_chunks = rows_per_batch // n
    fwd, bwd = _scan_maps(n_chunks, ctx_len // n)
    n_tok = mq.shape[0]

    def specs(row):
        return ([pl.BlockSpec((n, M_W), lambda b, i: (row(b, i), 0))] * 3
                + [pl.BlockSpec((n, LANES), lambda b, i: (row(b, i), 0))])

    return pl.pallas_call(
        _mlstm_kernel,
        grid=(n_batch, n_chunks),
        in_specs=specs(fwd) + specs(bwd),
        out_specs=[pl.BlockSpec((n, M_W), lambda b, i: (fwd(b, i), 0)),
                   pl.BlockSpec((n, M_W), lambda b, i: (bwd(b, i), 0))],
        out_shape=[jax.ShapeDtypeStruct((n_tok, M_W), BF16)] * 2,
        scratch_shapes=[pltpu.VMEM((2 * M_HEADS, M_HEAD_DIM, M_HEAD_DIM), F32),
                        pltpu.VMEM((2 * M_HEADS, 8, M_HEAD_DIM), F32),
                        pltpu.VMEM((2 * M_HEADS, 8, LANES), F32)],
        compiler_params=pltpu.CompilerParams(dimension_semantics=("arbitrary", "arbitrary"),
                                             vmem_limit_bytes=VMEM_LIMIT_BYTES),
        name="mlstm",
    )(mq, mk, mv, small, mq, mk, mv, small)


def _headnorm128(x, gain):
    parts = [_rmsnorm(x[:, s * LANES:(s + 1) * LANES], gain) for s in range(x.shape[1] // LANES)]
    return jnp.concatenate(parts, axis=1)


def _merge_kernel(x_ref, mod_ref, att_ref, gf_ref, gb_ref, hf_ref, hb_ref, gr_ref, mo_ref, gates_ref,
                  gn_ref, mn_ref, wa_ref, wg_ref, wm_ref, wo_ref, o_ref, *, tm, ctx_len):
    b, t = pl.program_id(0), pl.program_id(1)
    d = x_ref.shape[1]
    gla = gf_ref[...].astype(F32) + gb_ref[...].astype(F32)
    g = (_headnorm128(gla, gn_ref[...]) * gr_ref[...].astype(F32)).astype(BF16)
    mls = hf_ref[...].astype(F32) + hb_ref[...].astype(F32)
    m = (_headnorm128(mls, mn_ref[...]) * mo_ref[...].astype(F32)).astype(BF16)
    y = (gates_ref[:, 0:d].astype(F32) * _dot(att_ref[...], wa_ref[...])
         + gates_ref[:, d:2 * d].astype(F32) * _dot(g, wg_ref[...])
         + gates_ref[:, 2 * d:3 * d].astype(F32) * _dot(m, wm_ref[...]))
    _, _, gate = _mod_rows(mod_ref, 1, b, t, tm, ctx_len)
    o_ref[...] = x_ref[...] + gate * _dot(y.astype(BF16), wo_ref[...])


def _merge(xs, mod, att, gf, gb, hf, hb, gr, mo, gates, gn, mn, wa, wg, wm, wo, *, n_batch, rows_per_batch,
           ctx_len):
    tm = INPROJ_TM
    tpb = rows_per_batch // tm

    def tok(width):
        return pl.BlockSpec((tm, width), lambda b, t: (b * tpb + t, 0))

    streams = (xs, att, gf, gb, hf, hb, gr, mo, gates)
    consts = (gn, mn, wa, wg, wm, wo)
    in_specs = [tok(xs.shape[1]), _resident(mod.shape)] + [tok(a.shape[1]) for a in streams[1:]]
    in_specs += [_resident(a.shape) for a in consts]
    return pl.pallas_call(
        functools.partial(_merge_kernel, tm=tm, ctx_len=ctx_len),
        grid=(n_batch, tpb),
        in_specs=in_specs,
        out_specs=tok(xs.shape[1]),
        out_shape=jax.ShapeDtypeStruct(xs.shape, F32),
        compiler_params=pltpu.CompilerParams(dimension_semantics=("parallel", "parallel"),
                                             vmem_limit_bytes=VMEM_LIMIT_BYTES),
        name="merge",
    )(xs, mod, att, gf, gb, hf, hb, gr, mo, gates, *consts)


def _rope_tables(seq, ctx_len):
    rows = seq // GRID_W
    r = jnp.repeat(jnp.arange(rows), GRID_W).astype(F32)
    col = jnp.tile(jnp.arange(GRID_W), rows).astype(F32)
    n_freq = A_HEAD_DIM // 4
    inv = ROPE_BASE ** (-jnp.arange(n_freq, dtype=F32) / n_freq)
    ang = jnp.concatenate([r[:, None] * inv, col[:, None] * inv], axis=-1)
    cos, sin = jnp.cos(ang), jnp.sin(ang)
    cos = jnp.concatenate([cos, cos, cos, cos], axis=-1)
    sin = jnp.concatenate([-sin, sin, -sin, sin], axis=-1)
    cos = jnp.concatenate([jnp.ones((ctx_len, LANES), F32), cos], axis=0)
    sin = jnp.concatenate([jnp.zeros((ctx_len, LANES), F32), sin], axis=0)
    return cos, sin


def _reorder_w_in(w):
    idx = np.cumsum((0,) + IN_SPLITS)
    (aq, ak, av, gq, gk, gv, gr, gg, mq, mk, mv, mo, mi, mf, s_a, s_g, s_m) = [
        w[:, int(idx[i]):int(idx[i + 1])] for i in range(len(IN_SPLITS))]
    pad = jnp.zeros((w.shape[0], LANES - gg.shape[1] - mi.shape[1] - mf.shape[1]), w.dtype)
    return jnp.concatenate([aq, ak, av, gq, gk, gv, gr, mq, mk, mv, mo, s_a, s_g, s_m, gg, mi, mf, pad], axis=1)


def kernel(x, c, ctx, c_ctx, mod_w, mod_b, norm_g, ffn1_w13, ffn1_w2, ffn2_w13, ffn2_w2, w_in, attn_q_norm,
           attn_k_norm, attn_sink, gla_w2, gla_b, gla_norm, mlstm_conv_w, mlstm_conv_b, mlstm_ib, mlstm_fb,
           mlstm_norm, w_out_attn, w_out_gla, w_out_mlstm, w_o):
    n_batch, seq, d = x.shape
    ctx_len = ctx.shape[1]
    depth = mod_w.shape[0]
    rows_per_batch = ctx_len + seq
    assert d == D_MODEL and n_batch < MOD_ROWS
    assert rows_per_batch % FFN_TM == 0 and rows_per_batch % INPROJ_TM == 0
    assert ctx_len % MLSTM_CHUNK == 0 and seq % MLSTM_CHUNK == 0 and seq % GRID_W == 0
    geo = dict(n_batch=n_batch, rows_per_batch=rows_per_batch, ctx_len=ctx_len)

    xs = jnp.concatenate([ctx, x], axis=1).reshape(n_batch * rows_per_batch, d)
    cc = jnp.concatenate([c, jnp.broadcast_to(c_ctx[None], (MOD_ROWS - n_batch, d))], axis=0)
    mod_all = _modulation(cc, mod_w, mod_b)[:, :, :n_batch + 1]
    cos, sin = _rope_tables(seq, ctx_len)

    for l in range(depth):
        mod = mod_all[l]
        gains = norm_g[l]
        xs = _ffn(xs, mod, gains[0:1], ffn1_w13[l].astype(BF16), ffn1_w2[l].astype(BF16), j=0, **geo)

        wg = jnp.zeros((LANES, 2 * G_QK), F32)
        wg = wg.at[0:G_RANK, 0:G_QK].set(gla_w2[l, 0]).at[G_RANK:2 * G_RANK, G_QK:].set(gla_w2[l, 1])
        sb = jnp.zeros((1, LANES), F32)
        sb = sb.at[0, SMALL_MI:SMALL_MI + 2 * M_HEADS].set(mlstm_ib[l].reshape(-1))
        sb = sb.at[0, SMALL_MF:SMALL_MF + 2 * M_HEADS].set(mlstm_fb[l].reshape(-1))
        (q, k4, v4, gq, gk, gv, gr, mqk, mv, mo, gates, small, la) = _inproj(
            xs, mod, gains[1:2], _reorder_w_in(w_in[l]).astype(BF16), wg.astype(BF16),
            gla_b[l].reshape(1, 2 * G_QK), jnp.tile(attn_q_norm[l], 2)[None], jnp.tile(attn_k_norm[l], 2)[None],
            cos, sin, sb, **geo)

        mq, mk = _mconv(mqk, mlstm_conv_w[l], mlstm_conv_b[l][None], **geo)
        att = _attention(attn_sink[l], q, k4, v4, **geo)
        gf, gb = _gla(gq, gk, gv, la, **geo)
        hf, hb = _mlstm(mq, mk, mv, small, **geo)
        xs = _merge(xs, mod, att, gf, gb, hf, hb, gr, mo, gates, gla_norm[l][None], mlstm_norm[l][None],
                    w_out_attn[l].astype(BF16), w_out_gla[l].astype(BF16), w_out_mlstm[l].astype(BF16),
                    w_o[l].astype(BF16), **geo)
        xs = _ffn(xs, mod, gains[2:3], ffn2_w13[l].astype(BF16), ffn2_w2[l].astype(BF16), j=2, **geo)

    return xs.reshape(n_batch, rows_per_batch, d)[:, ctx_len:, :]
```

```python
import functools

import jax
import jax.numpy as jnp
import numpy as np
from jax import lax
from jax.experimental import pallas as pl
from jax.experimental.pallas import tpu as pltpu

F32 = jnp.float32
BF16 = jnp.bfloat16

D_MODEL = 1024
GRID_W = 64
A_HEADS, A_KV_HEADS, A_HEAD_DIM = 8, 2, 64
WINDOW = A_BLOCK = 128
ROPE_BASE = 10000.0
G_HEADS, G_DK, G_DV, G_RANK, G_TAU = 4, 64, 128, 16, 16.0
M_HEADS, M_HEAD_DIM, M_CONV = 4, 128, 5
D_FF = 2816
N_MOD = 9
EPS = 1e-6
A_Q = A_HEADS * A_HEAD_DIM
A_KV = A_KV_HEADS * A_HEAD_DIM
G_QK = G_HEADS * G_DK
G_V = G_HEADS * G_DV
M_W = M_HEADS * M_HEAD_DIM
IN_SPLITS = (A_Q, A_KV, A_KV, G_QK, G_QK, G_V, G_V, 2 * G_RANK,
             M_W, M_W, M_W, M_W, 2 * M_HEADS, 2 * M_HEADS, D_MODEL, D_MODEL, D_MODEL)

LANES = 128
BF16_SUBLANES = 16
VMEM_LIMIT_BYTES = 56 * 1024 * 1024

FFN_TM = 768
INPROJ_TM = 384
FFN_FK = 256
GLA_CHUNK = 128
MLSTM_CHUNK = 256
MOD_ROWS = 8

SMALL_GG = 0
SMALL_MI = 2 * G_RANK
SMALL_MF = SMALL_MI + 2 * M_HEADS

_OFF = {}
_o = 0
for _name, _w in (("aq", A_Q), ("akv", 2 * A_KV), ("gqk", 2 * G_QK), ("gv", G_V), ("gr", G_V),
                  ("mqk", 2 * M_W), ("mv", M_W), ("mo", M_W), ("gates", 3 * D_MODEL), ("small", LANES)):
    _OFF[_name] = (_o, _o + _w)
    _o += _w
D_IN_PAD = _o


def _dot(a, b):
    return jnp.dot(a, b, preferred_element_type=F32)


def _dot_nt(a, b):
    return lax.dot_general(a, b, (((1,), (1,)), ((), ())), preferred_element_type=F32)


def _dot_tn(a, b):
    return lax.dot_general(a, b, (((0,), (0,)), ((), ())), preferred_element_type=F32)


def _cumdot(tri, x):
    hi = x.astype(BF16)
    r1 = x - hi.astype(F32)
    mid = r1.astype(BF16)
    lo = (r1 - mid.astype(F32)).astype(BF16)
    return _dot(tri, hi) + _dot(tri, mid) + _dot(tri, lo)


def _silu(x):
    return x * jax.nn.sigmoid(x)


def _log_sigmoid(x):
    return jnp.minimum(x, 0.0) - jnp.log1p(jnp.exp(-jnp.abs(x)))


def _rmsnorm(x, gain):
    return x * lax.rsqrt(jnp.mean(x * x, axis=-1, keepdims=True) + EPS) * gain


def _mod_rows(mod_ref, j, b, t, tm, ctx_len):
    row = t * tm + lax.broadcasted_iota(jnp.int32, (tm, 1), 0)
    is_ctx = row < ctx_len
    n_b = mod_ref.shape[1] - 1
    out = []
    for q in range(3):
        lat = mod_ref[3 * j + q, pl.ds(b, 1), :]
        ctx = mod_ref[3 * j + q, n_b:n_b + 1, :]
        out.append(jnp.where(is_ctx, ctx, lat))
    return out


def _mod_kernel(c_ref, w_ref, b_ref, o_ref):
    h = _silu(c_ref[...]).astype(BF16)
    o_ref[...] = _dot(h, w_ref[...].astype(BF16)) + b_ref[...]


def _modulation(cc, mod_w, mod_b):
    depth, d, _ = mod_w.shape
    rows = cc.shape[0]
    return pl.pallas_call(
        _mod_kernel,
        grid=(depth, N_MOD),
        in_specs=[
            pl.BlockSpec((rows, d), lambda l, j: (0, 0)),
            pl.BlockSpec((None, d, d), lambda l, j: (l, 0, j)),
            pl.BlockSpec((None, None, 1, d), lambda l, j: (l, j, 0, 0)),
        ],
        out_specs=pl.BlockSpec((None, None, rows, d), lambda l, j: (l, j, 0, 0)),
        out_shape=jax.ShapeDtypeStruct((depth, N_MOD, rows, d), F32),
        compiler_params=pltpu.CompilerParams(dimension_semantics=("arbitrary", "arbitrary")),
        name="modulation",
    )(cc, mod_w, mod_b.reshape(depth, N_MOD, 1, d))


def _ffn_kernel(x_ref, mod_ref, g_ref, w13_ref, w2_ref, o_ref, h_ref, acc_ref, *, j, tm, ctx_len):
    b, t = pl.program_id(0), pl.program_id(1)
    x = x_ref[...]
    shift, scale, gate = _mod_rows(mod_ref, j, b, t, tm, ctx_len)
    h_ref[...] = (_rmsnorm(x, g_ref[...]) * (1.0 + scale) + shift).astype(BF16)
    for k in range(D_FF // FFN_FK):
        lo, hi = k * FFN_FK, (k + 1) * FFN_FK
        a = _dot(h_ref[...], w13_ref[:, lo:hi])
        g = _dot(h_ref[...], w13_ref[:, D_FF + lo:D_FF + hi])
        contrib = _dot((_silu(a) * g).astype(BF16), w2_ref[lo:hi, :])
        if k == 0:
            acc_ref[...] = contrib
        else:
            acc_ref[...] += contrib
    o_ref[...] = x + (0.5 * gate) * acc_ref[...]


def _resident(shape):
    return pl.BlockSpec(shape, lambda *_: (0,) * len(shape), pipeline_mode=pl.Buffered(1))


def _ffn(xs, mod, gain, w13, w2, *, j, n_batch, rows_per_batch, ctx_len):
    tm = FFN_TM
    tpb = rows_per_batch // tm
    d = xs.shape[1]
    tok = pl.BlockSpec((tm, d), lambda b, t: (b * tpb + t, 0))
    return pl.pallas_call(
        functools.partial(_ffn_kernel, j=j, tm=tm, ctx_len=ctx_len),
        grid=(n_batch, tpb),
        in_specs=[tok, _resident(mod.shape), _resident(gain.shape), _resident(w13.shape), _resident(w2.shape)],
        out_specs=tok,
        out_shape=jax.ShapeDtypeStruct(xs.shape, F32),
        scratch_shapes=[pltpu.VMEM((tm, d), BF16), pltpu.VMEM((tm, d), F32)],
        compiler_params=pltpu.CompilerParams(dimension_semantics=("parallel", "parallel"),
                                             vmem_limit_bytes=VMEM_LIMIT_BYTES),
        name=f"ffn{j}",
    )(xs, mod, gain, w13, w2)


def _headnorm_rope(z, gain, cos, sin):
    lane = lax.broadcasted_iota(jnp.int32, z.shape, 1)
    lo = lane < A_HEAD_DIM
    sq = z * z
    s_lo = jnp.sum(jnp.where(lo, sq, 0.0), axis=-1, keepdims=True)
    s_hi = jnp.sum(jnp.where(lo, 0.0, sq), axis=-1, keepdims=True)
    inv = lax.rsqrt(jnp.where(lo, s_lo, s_hi) * (1.0 / A_HEAD_DIM) + EPS)
    y = z * inv * gain
    first_half = (lane & (A_HEAD_DIM // 2)) == 0
    partner = jnp.where(first_half, pltpu.roll(y, LANES - A_HEAD_DIM // 2, 1), pltpu.roll(y, A_HEAD_DIM // 2, 1))
    return y * cos + partner * sin


def _spread_kv(z):
    lane = lax.broadcasted_iota(jnp.int32, z.shape, 1)
    lo = lane < A_HEAD_DIM
    a0 = jnp.where(lo, z, 0.0)
    b1 = jnp.where(lo, 0.0, z)
    return [a0, pltpu.roll(a0, A_HEAD_DIM, 1), pltpu.roll(b1, A_HEAD_DIM, 1), b1]


def _inproj_kernel(x_ref, mod_ref, g_ref, w_ref, wg_ref, bg_ref, qn_ref, kn_ref, cos_ref, sin_ref, sb_ref,
                   q_o, k4_o, v4_o, gq_o, gk_o, gv_o, gr_o, mqk_o, mv_o, mo_o, gates_o, small_o, la_o,
                   h_ref, *, tm, ctx_len):
    b, t = pl.program_id(0), pl.program_id(1)
    shift, scale, _ = _mod_rows(mod_ref, 1, b, t, tm, ctx_len)
    h_ref[...] = (_rmsnorm(x_ref[...], g_ref[...]) * (1.0 + scale) + shift).astype(BF16)

    def proj(name, lo, hi):
        base = _OFF[name][0]
        return _dot(h_ref[...], w_ref[:, base + lo:base + hi])

    cos, sin = cos_ref[...], sin_ref[...]
    z = proj("aq", 0, A_Q)
    for s in range(A_Q // LANES):
        y = _headnorm_rope(z[:, s * LANES:(s + 1) * LANES], qn_ref[...], cos, sin)
        q_o[:, s * LANES:(s + 1) * LANES] = (y * (A_HEAD_DIM ** -0.5)).astype(BF16)
    z = proj("akv", 0, 2 * A_KV)
    for s, part in enumerate(_spread_kv(_headnorm_rope(z[:, :A_KV], kn_ref[...], cos, sin))):
        k4_o[:, s * LANES:(s + 1) * LANES] = part.astype(BF16)
    for s, part in enumerate(_spread_kv(z[:, A_KV:])):
        v4_o[:, s * LANES:(s + 1) * LANES] = part.astype(BF16)
    z = proj("gqk", 0, 2 * G_QK)
    gq_o[...] = (z[:, :G_QK] * (G_DK ** -0.5)).astype(BF16)
    gk_o[...] = z[:, G_QK:].astype(BF16)
    gv_o[...] = proj("gv", 0, G_V).astype(BF16)
    gr_o[...] = _silu(proj("gr", 0, G_V)).astype(BF16)
    mqk_o[:, :M_W] = proj("mqk", 0, M_W).astype(BF16)
    mqk_o[:, M_W:] = proj("mqk", M_W, 2 * M_W).astype(BF16)
    mv_o[...] = proj("mv", 0, M_W).astype(BF16)
    mo_o[...] = jax.nn.sigmoid(proj("mo", 0, M_W)).astype(BF16)
    for s in range(3 * D_MODEL // 512):
        gates_o[:, s * 512:(s + 1) * 512] = jax.nn.sigmoid(proj("gates", s * 512, (s + 1) * 512)).astype(BF16)
    zs = proj("small", 0, LANES)
    lane = lax.broadcasted_iota(jnp.int32, zs.shape, 1)
    zb = zs + sb_ref[...]
    is_f = (lane >= SMALL_MF) & (lane < SMALL_MF + 2 * M_HEADS)
    small_o[...] = jnp.where(is_f, _log_sigmoid(zb), zb)
    la_o[...] = _log_sigmoid(_dot(zs.astype(BF16), wg_ref[...]) + bg_ref[...]) * (1.0 / G_TAU)


def _inproj(xs, mod, gain, w_in, wg, bg, qn, kn, cos, sin, sb, *, n_batch, rows_per_batch, ctx_len):
    tm = INPROJ_TM
    tpb = rows_per_batch // tm
    n_tok, d = xs.shape

    def tok(width):
        return pl.BlockSpec((tm, width), lambda b, t: (b * tpb + t, 0))

    pos = pl.BlockSpec((tm, LANES), lambda b, t: (t, 0))
    widths = (A_Q, 4 * LANES, 4 * LANES, G_QK, G_QK, G_V, G_V, 2 * M_W, M_W, M_W, 3 * D_MODEL)
    out_shape = [jax.ShapeDtypeStruct((n_tok, w), BF16) for w in widths]
    out_shape += [jax.ShapeDtypeStruct((n_tok, LANES), F32), jax.ShapeDtypeStruct((n_tok, 2 * G_QK), F32)]
    out_specs = [tok(w) for w in widths] + [tok(LANES), tok(2 * G_QK)]
    return pl.pallas_call(
        functools.partial(_inproj_kernel, tm=tm, ctx_len=ctx_len),
        grid=(n_batch, tpb),
        in_specs=[tok(d), _resident(mod.shape), _resident(gain.shape), _resident(w_in.shape),
                  _resident(wg.shape), _resident(bg.shape), _resident(qn.shape), _resident(kn.shape),
                  pos, pos, _resident(sb.shape)],
        out_specs=out_specs,
        out_shape=out_shape,
        scratch_shapes=[pltpu.VMEM((tm, d), BF16)],
        compiler_params=pltpu.CompilerParams(dimension_semantics=("parallel", "parallel"),
                                             vmem_limit_bytes=VMEM_LIMIT_BYTES),
        name="inproj",
    )(xs, mod, gain, w_in, wg, bg, qn, kn, cos, sin, sb)


def _mconv_kernel(prev_ref, cur_ref, next_ref, w_ref, b_ref, q_o, k_o, e_ref, *, tm, rows_per_batch, ctx_len):
    t = pl.program_id(1)
    halo = BF16_SUBLANES
    e_ref[0:halo, :] = prev_ref[...].astype(F32)
    e_ref[halo:halo + tm, :] = cur_ref[...].astype(F32)
    e_ref[halo + tm:, :] = next_ref[...].astype(F32)
    row = t * tm + lax.broadcasted_iota(jnp.int32, (tm, 1), 0)
    pad = M_CONV // 2
    acc = None
    for tap in range(M_CONV):
        off = tap - pad
        nb = row + off
        valid = (nb >= 0) & (nb < rows_per_batch) & ((row < ctx_len) == (nb < ctx_len))
        term = jnp.where(valid, e_ref[pl.ds(halo + off, tm), :], 0.0) * w_ref[tap:tap + 1, :]
        acc = term if acc is None else acc + term
    y = _silu(acc + b_ref[...])
    q_o[...] = y[:, :M_W].astype(BF16)
    k_o[...] = (y[:, M_W:] * (M_HEAD_DIM ** -0.5)).astype(BF16)


def _mconv(mqk, w, bias, *, n_batch, rows_per_batch, ctx_len):
    tm = FFN_TM
    tpb = rows_per_batch // tm
    n_tok, width = mqk.shape
    hb = tm // BF16_SUBLANES
    n_hb = n_tok // BF16_SUBLANES
    cur = pl.BlockSpec((tm, width), lambda b, t: (b * tpb + t, 0))
    prev = pl.BlockSpec((BF16_SUBLANES, width), lambda b, t: (jnp.maximum((b * tpb + t) * hb - 1, 0), 0))
    nxt = pl.BlockSpec((BF16_SUBLANES, width), lambda b, t: (jnp.minimum((b * tpb + t + 1) * hb, n_hb - 1), 0))
    half = pl.BlockSpec((tm, M_W), lambda b, t: (b * tpb + t, 0))
    return pl.pallas_call(
        functools.partial(_mconv_kernel, tm=tm, rows_per_batch=rows_per_batch, ctx_len=ctx_len),
        grid=(n_batch, tpb),
        in_specs=[prev, cur, nxt, _resident(w.shape), _resident(bias.shape)],
        out_specs=[half, half],
        out_shape=[jax.ShapeDtypeStruct((n_tok, M_W), BF16)] * 2,
        scratch_shapes=[pltpu.VMEM((tm + 2 * BF16_SUBLANES, width), F32)],
        compiler_params=pltpu.CompilerParams(dimension_semantics=("parallel", "parallel"),
                                             vmem_limit_bytes=VMEM_LIMIT_BYTES),
        name="mconv",
    )(mqk, mqk, mqk, w, bias)


def _attn_kernel(sink_ref, q_ref, kp_ref, kc_ref, kn_ref, kx_ref, vp_ref, vc_ref, vn_ref, vx_ref, o_ref,
                 *, n_ctx_blk, n_blk):
    j = pl.program_id(1)
    blk = A_BLOCK
    off = jnp.int32(4 * blk)
    cur_off = jnp.where(j >= n_ctx_blk, 0, off)
    prev_off = jnp.where(j >= n_ctx_blk + 1, 0, off)
    next_off = jnp.where((j >= n_ctx_blk) & (j <= n_blk - 2), 0, off)
    r = lax.broadcasted_iota(jnp.int32, (blk, blk), 0)
    c = lax.broadcasted_iota(jnp.int32, (blk, blk), 1)
    neg = -jnp.inf
    n_ctx = kx_ref.shape[0]
    bias = jnp.concatenate([
        jnp.where(c >= r + prev_off, 0.0, neg),
        jnp.where(c >= cur_off, 0.0, neg),
        jnp.where(c <= r - next_off, 0.0, neg),
        jnp.zeros((blk, n_ctx), F32)], axis=1)
    bias = jnp.concatenate([bias, bias], axis=0)
    top = lax.broadcasted_iota(jnp.int32, (2 * blk, 1), 0) < blk
    k_all = jnp.concatenate([kp_ref[...], kc_ref[...], kn_ref[...], kx_ref[...]], axis=0)
    v_all = jnp.concatenate([vp_ref[...], vc_ref[...], vn_ref[...], vx_ref[...]], axis=0)
    heads_per_kv = A_HEADS // A_KV_HEADS
    combos = [(g, half) for g in range(A_KV_HEADS) for half in range(2)]
    p, inv = {}, {}
    for g, half in combos:
        c0 = g * 2 * LANES
        qg = jnp.concatenate([q_ref[:, c0:c0 + LANES], q_ref[:, c0 + LANES:c0 + 2 * LANES]], axis=0)
        col = (2 * g + half) * LANES
        s = _dot_nt(qg, k_all[:, col:col + LANES]) + bias
        sink = jnp.where(top, sink_ref[heads_per_kv * g + half], sink_ref[heads_per_kv * g + 2 + half])
        m = jnp.maximum(jnp.max(s, axis=-1, keepdims=True), sink)
        e = jnp.exp(s - m)
        inv[g, half] = 1.0 / (jnp.sum(e, axis=-1, keepdims=True) + jnp.exp(sink - m))
        p[g, half] = e.astype(BF16)
    for g in range(A_KV_HEADS):
        c0 = g * 2 * LANES
        acc = None
        for half in range(2):
            col = (2 * g + half) * LANES
            o = _dot(p[g, half], v_all[:, col:col + LANES]) * inv[g, half]
            acc = o if acc is None else acc + o
        o_ref[:, c0:c0 + LANES] = acc[:blk].astype(BF16)
        o_ref[:, c0 + LANES:c0 + 2 * LANES] = acc[blk:].astype(BF16)


def _attention(sink, q, k4, v4, *, n_batch, rows_per_batch, ctx_len):
    blk = A_BLOCK
    n_blk = rows_per_batch // blk
    n_ctx_blk = ctx_len // blk
    n_tok, width = q.shape
    cur = lambda b, j: (b * n_blk + j, 0)
    prev = lambda b, j: (b * n_blk + jnp.maximum(j - 1, 0), 0)
    nxt = lambda b, j: (b * n_blk + jnp.minimum(j + 1, n_blk - 1), 0)
    ctx = lambda b, j: (b * (rows_per_batch // ctx_len), 0)
    kv_specs = [pl.BlockSpec((blk, width), prev), pl.BlockSpec((blk, width), cur),
                pl.BlockSpec((blk, width), nxt), pl.BlockSpec((ctx_len, width), ctx)]
    return pl.pallas_call(
        functools.partial(_attn_kernel, n_ctx_blk=n_ctx_blk, n_blk=n_blk),
        grid=(n_batch, n_blk),
        in_specs=[pl.BlockSpec(memory_space=pltpu.SMEM), pl.BlockSpec((blk, width), cur)] + kv_specs + kv_specs,
        out_specs=pl.BlockSpec((blk, width), cur),
        out_shape=jax.ShapeDtypeStruct((n_tok, width), BF16),
        compiler_params=pltpu.CompilerParams(dimension_semantics=("parallel", "parallel"),
                                             vmem_limit_bytes=VMEM_LIMIT_BYTES),
        name="attention",
    )(sink, q, k4, k4, k4, k4, v4, v4, v4, v4)


def _gla_direction(q_ref, k_ref, v_ref, la_ref, o_ref, st_ref, *, reverse):
    n = q_ref.shape[0]
    r = lax.broadcasted_iota(jnp.int32, (n, n), 0)
    c = lax.broadcasted_iota(jnp.int32, (n, n), 1)
    keep = (c >= r) if reverse else (c <= r)
    last = 0 if reverse else n - 1
    tri = jnp.where(keep, 1.0, 0.0).astype(BF16)
    cum = _cumdot(tri, la_ref[...])
    mid = cum[n // 2:n // 2 + 1, :]
    end = cum[last:last + 1, :]
    qf, kf = q_ref[...].astype(F32), k_ref[...].astype(F32)
    q_in = (qf * jnp.exp(cum - mid)).astype(BF16)
    k_in = (kf * jnp.exp(mid - cum)).astype(BF16)
    q_st = (qf * jnp.exp(cum)).astype(BF16)
    k_st = (kf * jnp.exp(end - cum)).astype(BF16)
    decay = jnp.exp(end)
    lane = lax.broadcasted_iota(jnp.int32, (n, LANES), 1)
    lo = lane < G_DK
    zero = jnp.zeros((), BF16)
    rr = lax.broadcasted_iota(jnp.int32, (2 * G_DV, 2 * G_DK), 0)
    cc = lax.broadcasted_iota(jnp.int32, (2 * G_DV, 2 * G_DK), 1)
    diag = (rr < G_DV) == (cc < G_DK)
    slabs = range(G_HEADS // 2)
    sls = [slice(s * LANES, (s + 1) * LANES) for s in slabs]
    yield
    a = []
    for s in slabs:
        qs, ks = q_in[:, sls[s]], k_in[:, sls[s]]
        a.append((jnp.where(keep, _dot_nt(qs, jnp.where(lo, ks, zero)), 0.0).astype(BF16),
                  jnp.where(keep, _dot_nt(qs, jnp.where(lo, zero, ks)), 0.0).astype(BF16)))
    yield
    for s in slabs:
        inter = _dot_nt(q_st[:, sls[s]], st_ref[s].astype(BF16))
        for half in range(2):
            vsl = slice((2 * s + half) * G_DV, (2 * s + half + 1) * G_DV)
            o_ref[:, vsl] = (_dot(a[s][half], v_ref[:, vsl])
                             + inter[:, half * G_DV:(half + 1) * G_DV]).astype(o_ref.dtype)
    yield
    for s in slabs:
        upd = _dot_tn(v_ref[:, 2 * s * G_DV:(2 * s + 2) * G_DV], k_st[:, sls[s]])
        st_ref[s] = decay[:, sls[s]] * st_ref[s] + jnp.where(diag, upd, 0.0)
    yield


def _gla_kernel(qf_ref, kf_ref, vf_ref, laf_ref, qb_ref, kb_ref, vb_ref, lab_ref, of_ref, ob_ref, sf_ref, sb_ref):
    @pl.when(pl.program_id(1) == 0)
    def _():
        sf_ref[...] = jnp.zeros_like(sf_ref)
        sb_ref[...] = jnp.zeros_like(sb_ref)

    fwd = _gla_direction(qf_ref, kf_ref, vf_ref, laf_ref, of_ref, sf_ref, reverse=False)
    bwd = _gla_direction(qb_ref, kb_ref, vb_ref, lab_ref, ob_ref, sb_ref, reverse=True)
    for _ in zip(fwd, bwd):
        pass


def _scan_maps(n_chunks, n_ctx_chunks):
    def fwd(b, i):
        return b * n_chunks + i

    def bwd(b, i):
        return b * n_chunks + jnp.where(i < n_ctx_chunks, n_ctx_chunks - 1 - i, n_chunks - 1 + n_ctx_chunks - i)

    return fwd, bwd


def _gla(gq, gk, gv, la, *, n_batch, rows_per_batch, ctx_len):
    n = GLA_CHUNK
    n_chunks = rows_per_batch // n
    fwd, bwd = _scan_maps(n_chunks, ctx_len // n)
    n_tok = gq.shape[0]

    def specs(row):
        return [pl.BlockSpec((n, G_QK), lambda b, i: (row(b, i), 0)),
                pl.BlockSpec((n, G_QK), lambda b, i: (row(b, i), 0)),
                pl.BlockSpec((n, G_V), lambda b, i: (row(b, i), 0))]

    state = pltpu.VMEM((G_HEADS // 2, 2 * G_DV, 2 * G_DK), F32)
    return pl.pallas_call(
        _gla_kernel,
        grid=(n_batch, n_chunks),
        in_specs=(specs(fwd) + [pl.BlockSpec((n, G_QK), lambda b, i: (fwd(b, i), 0))]
                  + specs(bwd) + [pl.BlockSpec((n, G_QK), lambda b, i: (bwd(b, i), 1))]),
        out_specs=[pl.BlockSpec((n, G_V), lambda b, i: (fwd(b, i), 0)),
                   pl.BlockSpec((n, G_V), lambda b, i: (bwd(b, i), 0))],
        out_shape=[jax.ShapeDtypeStruct((n_tok, G_V), BF16)] * 2,
        scratch_shapes=[state, state],
        compiler_params=pltpu.CompilerParams(dimension_semantics=("arbitrary", "arbitrary"),
                                             vmem_limit_bytes=VMEM_LIMIT_BYTES),
        name="gla",
    )(gq, gk, gv, la, gq, gk, gv, la)


def _cumdot_right(x, tri):
    hi = x.astype(BF16)
    r1 = x - hi.astype(F32)
    mid = r1.astype(BF16)
    lo = (r1 - mid.astype(F32)).astype(BF16)
    return _dot(hi, tri) + _dot(mid, tri) + _dot(lo, tri)


def _mlstm_direction(q_ref, k_ref, v_ref, small_ref, o_ref, ct_ref, n_ref, m_ref, *, direction):
    n = q_ref.shape[0]
    reverse = direction == 1
    n_gates = 2 * M_HEADS
    jj = lax.broadcasted_iota(jnp.int32, (n, n), 0)
    ii = lax.broadcasted_iota(jnp.int32, (n, n), 1)
    vis = (jj >= ii) if reverse else (jj <= ii)
    last = 0 if reverse else n - 1
    vis_bf = jnp.where(vis, 1.0, 0.0).astype(BF16)
    gates_t = small_ref[...].T[SMALL_MI:SMALL_MI + 2 * n_gates, :]
    b_rows = _cumdot_right(gates_t, vis_bf)[n_gates:, :]
    u_rows = gates_t[:n_gates, :] - b_rows
    m_prev = m_ref[direction][:, 0:1]
    lane = lax.broadcasted_iota(jnp.int32, (n_gates, n), 1)
    m_rel = u_rows
    for step in range(n.bit_length() - 1):
        s = 1 << step
        if reverse:
            shifted = jnp.where(lane < n - s, pltpu.roll(m_rel, n - s, 1), -jnp.inf)
        else:
            shifted = jnp.where(lane >= s, pltpu.roll(m_rel, s, 1), -jnp.inf)
        m_rel = jnp.maximum(m_rel, shifted)
    m_rel = jnp.maximum(m_rel, m_prev)
    w_inter = jnp.exp(m_prev - m_rel)
    m_rows = b_rows + m_rel
    inv_floor = jnp.exp(-m_rows)
    m_new = m_rows[:, last:last + 1]
    b_end = b_rows[:, last:last + 1]
    decay = jnp.exp(b_end + m_prev - m_new)
    m_ref[direction] = jnp.broadcast_to(m_new, m_ref.shape[1:])
    log2e = 1.4426950408889634
    m2_rows = m_rel * log2e
    c2 = (b_end - m_new) * log2e
    u2_cols = jnp.concatenate([u_rows * log2e, jnp.zeros((LANES - n_gates, n), F32)], axis=0).T
    v_t = v_ref[...].T
    ones = jnp.ones((8, n), BF16)
    heads = range(M_HEADS)
    sls = [slice(h * M_HEAD_DIM, (h + 1) * M_HEAD_DIM) for h in heads]
    rows = [slice(direction * M_HEADS + h, direction * M_HEADS + h + 1) for h in heads]
    u2, s_t = [], []
    for h in heads:
        u2.append(jnp.broadcast_to(u2_cols[:, rows[h]], (n, n)))
        w_t = jnp.exp2(jnp.where(vis, u2[h] - m2_rows[rows[h], :], -jnp.inf))
        s_t.append((_dot_nt(k_ref[:, sls[h]], q_ref[:, sls[h]]) * w_t).astype(BF16))
    yield
    for h in heads:
        idx = direction * M_HEADS + h
        qh = q_ref[:, sls[h]]
        nq = _dot_nt(n_ref[idx].astype(BF16), qh)[0:1, :]
        den = w_inter[rows[h], :] * nq + _dot(ones, s_t[h])[0:1, :]
        norm = 1.0 / jnp.maximum(jnp.abs(den), inv_floor[rows[h], :])
        h_t = (_dot(v_t[sls[h], :], s_t[h])
               + _dot_nt(ct_ref[idx].astype(BF16), qh) * w_inter[rows[h], :]) * norm
        o_ref[:, sls[h]] = h_t.astype(o_ref.dtype).T
    yield
    for h in heads:
        idx = direction * M_HEADS + h
        kw = k_ref[:, sls[h]].astype(F32) * jnp.exp2(u2[h][:, :M_HEAD_DIM] + c2[rows[h], :])
        ct_ref[idx] = decay[rows[h], :] * ct_ref[idx] + _dot(v_t[sls[h], :], kw.astype(BF16))
        n_new = decay[rows[h], :] * n_ref[idx][0:1, :] + jnp.sum(kw, axis=0, keepdims=True)
        n_ref[idx] = jnp.broadcast_to(n_new, n_ref.shape[1:])
    yield


def _mlstm_kernel(qf_ref, kf_ref, vf_ref, sf_ref, qb_ref, kb_ref, vb_ref, sb_ref, of_ref, ob_ref,
                  c_ref, n_ref, m_ref):
    @pl.when(pl.program_id(1) == 0)
    def _():
        c_ref[...] = jnp.zeros_like(c_ref)
        n_ref[...] = jnp.zeros_like(n_ref)
        m_ref[...] = jnp.zeros_like(m_ref)

    fwd = _mlstm_direction(qf_ref, kf_ref, vf_ref, sf_ref, of_ref, c_ref, n_ref, m_ref, direction=0)
    bwd = _mlstm_direction(qb_ref, kb_ref, vb_ref, sb_ref, ob_ref, c_ref, n_ref, m_ref, direction=1)
    for _ in zip(fwd, bwd):
        pass


def _mlstm(mq, mk, mv, small, *, n_batch, rows_per_batch, ctx_len):
    n = MLSTM_CHUNK
    n_chunks = rows_per_batch // n
    fwd, bwd = _scan_maps(n_chunks, ctx_len // n)
    n_tok = mq.shape[0]

    def specs(row):
        return ([pl.BlockSpec((n, M_W), lambda b, i: (row(b, i), 0))] * 3
                + [pl.BlockSpec((n, LANES), lambda b, i: (row(b, i), 0))])

    return pl.pallas_call(
        _mlstm_kernel,
        grid=(n_batch, n_chunks),
        in_specs=specs(fwd) + specs(bwd),
        out_specs=[pl.BlockSpec((n, M_W), lambda b, i: (fwd(b, i), 0)),
                   pl.BlockSpec((n, M_W), lambda b, i: (bwd(b, i), 0))],
        out_shape=[jax.ShapeDtypeStruct((n_tok, M_W), BF16)] * 2,
        scratch_shapes=[pltpu.VMEM((2 * M_HEADS, M_HEAD_DIM, M_HEAD_DIM), F32),
                        pltpu.VMEM((2 * M_HEADS, 8, M_HEAD_DIM), F32),
                        pltpu.VMEM((2, 2 * M_HEADS, LANES), F32)],
        compiler_params=pltpu.CompilerParams(dimension_semantics=("arbitrary", "arbitrary"),
                                             vmem_limit_bytes=VMEM_LIMIT_BYTES),
        name="mlstm",
    )(mq, mk, mv, small, mq, mk, mv, small)


def _headnorm128(x, gain):
    parts = [_rmsnorm(x[:, s * LANES:(s + 1) * LANES], gain) for s in range(x.shape[1] // LANES)]
    return jnp.concatenate(parts, axis=1)


def _merge_kernel(x_ref, mod_ref, att_ref, gf_ref, gb_ref, hf_ref, hb_ref, gr_ref, mo_ref, gates_ref,
                  gn_ref, mn_ref, wa_ref, wg_ref, wm_ref, wo_ref, o_ref, *, tm, ctx_len):
    b, t = pl.program_id(0), pl.program_id(1)
    d = x_ref.shape[1]
    gla = gf_ref[...].astype(F32) + gb_ref[...].astype(F32)
    g = (_headnorm128(gla, gn_ref[...]) * gr_ref[...].astype(F32)).astype(BF16)
    mls = hf_ref[...].astype(F32) + hb_ref[...].astype(F32)
    m = (_headnorm128(mls, mn_ref[...]) * mo_ref[...].astype(F32)).astype(BF16)
    y = (gates_ref[:, 0:d].astype(F32) * _dot(att_ref[...], wa_ref[...])
         + gates_ref[:, d:2 * d].astype(F32) * _dot(g, wg_ref[...])
         + gates_ref[:, 2 * d:3 * d].astype(F32) * _dot(m, wm_ref[...]))
    _, _, gate = _mod_rows(mod_ref, 1, b, t, tm, ctx_len)
    o_ref[...] = x_ref[...] + gate * _dot(y.astype(BF16), wo_ref[...])


def _merge(xs, mod, att, gf, gb, hf, hb, gr, mo, gates, gn, mn, wa, wg, wm, wo, *, n_batch, rows_per_batch,
           ctx_len):
    tm = INPROJ_TM
    tpb = rows_per_batch // tm

    def tok(width):
        return pl.BlockSpec((tm, width), lambda b, t: (b * tpb + t, 0))

    streams = (xs, att, gf, gb, hf, hb, gr, mo, gates)
    consts = (gn, mn, wa, wg, wm, wo)
    in_specs = [tok(xs.shape[1]), _resident(mod.shape)] + [tok(a.shape[1]) for a in streams[1:]]
    in_specs += [_resident(a.shape) for a in consts]
    return pl.pallas_call(
        functools.partial(_merge_kernel, tm=tm, ctx_len=ctx_len),
        grid=(n_batch, tpb),
        in_specs=in_specs,
        out_specs=tok(xs.shape[1]),
        out_shape=jax.ShapeDtypeStruct(xs.shape, F32),
        compiler_params=pltpu.CompilerParams(dimension_semantics=("parallel", "parallel"),
                                             vmem_limit_bytes=VMEM_LIMIT_BYTES),
        name="merge",
    )(xs, mod, att, gf, gb, hf, hb, gr, mo, gates, *consts)


def _rope_tables(seq, ctx_len):
    rows = seq // GRID_W
    r = jnp.repeat(jnp.arange(rows), GRID_W).astype(F32)
    col = jnp.tile(jnp.arange(GRID_W), rows).astype(F32)
    n_freq = A_HEAD_DIM // 4
    inv = ROPE_BASE ** (-jnp.arange(n_freq, dtype=F32) / n_freq)
    ang = jnp.concatenate([r[:, None] * inv, col[:, None] * inv], axis=-1)
    cos, sin = jnp.cos(ang), jnp.sin(ang)
    cos = jnp.concatenate([cos, cos, cos, cos], axis=-1)
    sin = jnp.concatenate([-sin, sin, -sin, sin], axis=-1)
    cos = jnp.concatenate([jnp.ones((ctx_len, LANES), F32), cos], axis=0)
    sin = jnp.concatenate([jnp.zeros((ctx_len, LANES), F32), sin], axis=0)
    return cos, sin


def _reorder_w_in(w):
    idx = np.cumsum((0,) + IN_SPLITS)
    (aq, ak, av, gq, gk, gv, gr, gg, mq, mk, mv, mo, mi, mf, s_a, s_g, s_m) = [
        w[:, int(idx[i]):int(idx[i + 1])] for i in range(len(IN_SPLITS))]
    pad = jnp.zeros((w.shape[0], LANES - gg.shape[1] - mi.shape[1] - mf.shape[1]), w.dtype)
    return jnp.concatenate([aq, ak, av, gq, gk, gv, gr, mq, mk, mv, mo, s_a, s_g, s_m, gg, mi, mf, pad], axis=1)


def kernel(x, c, ctx, c_ctx, mod_w, mod_b, norm_g, ffn1_w13, ffn1_w2, ffn2_w13, ffn2_w2, w_in, attn_q_norm,
           attn_k_norm, attn_sink, gla_w2, gla_b, gla_norm, mlstm_conv_w, mlstm_conv_b, mlstm_ib, mlstm_fb,
           mlstm_norm, w_out_attn, w_out_gla, w_out_mlstm, w_o):
    n_batch, seq, d = x.shape
    ctx_len = ctx.shape[1]
    depth = mod_w.shape[0]
    rows_per_batch = ctx_len + seq
    assert d == D_MODEL and n_batch < MOD_ROWS
    assert rows_per_batch % FFN_TM == 0 and rows_per_batch % INPROJ_TM == 0
    assert ctx_len % MLSTM_CHUNK == 0 and seq % MLSTM_CHUNK == 0 and seq % GRID_W == 0
    geo = dict(n_batch=n_batch, rows_per_batch=rows_per_batch, ctx_len=ctx_len)

    xs = jnp.concatenate([ctx, x], axis=1).reshape(n_batch * rows_per_batch, d)
    cc = jnp.concatenate([c, jnp.broadcast_to(c_ctx[None], (MOD_ROWS - n_batch, d))], axis=0)
    mod_all = _modulation(cc, mod_w, mod_b)[:, :, :n_batch + 1]
    cos, sin = _rope_tables(seq, ctx_len)

    for l in range(depth):
        mod = mod_all[l]
        gains = norm_g[l]
        xs = _ffn(xs, mod, gains[0:1], ffn1_w13[l].astype(BF16), ffn1_w2[l].astype(BF16), j=0, **geo)

        wg = jnp.zeros((LANES, 2 * G_QK), F32)
        wg = wg.at[0:G_RANK, 0:G_QK].set(gla_w2[l, 0]).at[G_RANK:2 * G_RANK, G_QK:].set(gla_w2[l, 1])
        sb = jnp.zeros((1, LANES), F32)
        sb = sb.at[0, SMALL_MI:SMALL_MI + 2 * M_HEADS].set(mlstm_ib[l].reshape(-1))
        sb = sb.at[0, SMALL_MF:SMALL_MF + 2 * M_HEADS].set(mlstm_fb[l].reshape(-1))
        (q, k4, v4, gq, gk, gv, gr, mqk, mv, mo, gates, small, la) = _inproj(
            xs, mod, gains[1:2], _reorder_w_in(w_in[l]).astype(BF16), wg.astype(BF16),
            gla_b[l].reshape(1, 2 * G_QK), jnp.tile(attn_q_norm[l], 2)[None], jnp.tile(attn_k_norm[l], 2)[None],
            cos, sin, sb, **geo)

        mq, mk = _mconv(mqk, mlstm_conv_w[l], mlstm_conv_b[l][None], **geo)
        att = _attention(attn_sink[l], q, k4, v4, **geo)
        gf, gb = _gla(gq, gk, gv, la, **geo)
        hf, hb = _mlstm(mq, mk, mv, small, **geo)
        xs = _merge(xs, mod, att, gf, gb, hf, hb, gr, mo, gates, gla_norm[l][None], mlstm_norm[l][None],
                    w_out_attn[l].astype(BF16), w_out_gla[l].astype(BF16), w_out_mlstm[l].astype(BF16),
                    w_o[l].astype(BF16), **geo)
        xs = _ffn(xs, mod, gains[2:3], ffn2_w13[l].astype(BF16), ffn2_w2[l].astype(BF16), j=2, **geo)

    return xs.reshape(n_batch, rows_per_batch, d)[:, ctx_len:, :]
```

```python
import functools
import math

import jax
import jax.numpy as jnp
import numpy as np
from jax import lax
from jax.experimental import pallas as pl
from jax.experimental.pallas import tpu as pltpu

F32 = jnp.float32
BF16 = jnp.bfloat16

D_MODEL = 1024
GRID_W = 64
A_HEADS, A_KV_HEADS, A_HEAD_DIM = 8, 2, 64
WINDOW = A_BLOCK = 128
ROPE_BASE = 10000.0
G_HEADS, G_DK, G_DV, G_RANK, G_TAU = 4, 64, 128, 16, 16.0
M_HEADS, M_HEAD_DIM, M_CONV = 4, 128, 5
D_FF = 2816
N_MOD = 9
EPS = 1e-6
A_Q = A_HEADS * A_HEAD_DIM
A_KV = A_KV_HEADS * A_HEAD_DIM
G_QK = G_HEADS * G_DK
G_V = G_HEADS * G_DV
M_W = M_HEADS * M_HEAD_DIM
IN_SPLITS = (A_Q, A_KV, A_KV, G_QK, G_QK, G_V, G_V, 2 * G_RANK,
             M_W, M_W, M_W, M_W, 2 * M_HEADS, 2 * M_HEADS, D_MODEL, D_MODEL, D_MODEL)

LANES = 128
F32_SUBLANES = 8
VMEM_LIMIT_BYTES = 56 * 1024 * 1024

FFN_TM = 768
INPROJ_TM = 384
MERGE_TM_LATENT = 512
FFN_FK = 256
GLA_CHUNK = 128
MLSTM_CHUNK = 256
SCAN_BATCH_GROUP = 2
MOD_ROWS = 8

SMALL_GG = 0
SMALL_MI = 2 * G_RANK
SMALL_MF = SMALL_MI + 2 * M_HEADS

_OFF = {}
_o = 0
for _name, _w in (("aq", A_Q), ("akv", 2 * A_KV), ("gqk", 2 * G_QK), ("gv", G_V), ("gr", G_V),
                  ("mqk", 2 * M_W), ("mv", M_W), ("mo", M_W), ("gates", 3 * D_MODEL), ("small", LANES)):
    _OFF[_name] = (_o, _o + _w)
    _o += _w
D_IN_PAD = _o


def _dot(a, b):
    return jnp.dot(a, b, preferred_element_type=F32)


def _dot_nt(a, b):
    return lax.dot_general(a, b, (((1,), (1,)), ((), ())), preferred_element_type=F32)


def _dot_tn(a, b):
    return lax.dot_general(a, b, (((0,), (0,)), ((), ())), preferred_element_type=F32)


def _cumdot(tri, x):
    hi = x.astype(BF16)
    r1 = x - hi.astype(F32)
    mid = r1.astype(BF16)
    lo = (r1 - mid.astype(F32)).astype(BF16)
    return _dot(tri, hi) + _dot(tri, mid) + _dot(tri, lo)


def _silu(x):
    return x * jax.nn.sigmoid(x)


def _log_sigmoid(x):
    return jnp.minimum(x, 0.0) - jnp.log1p(jnp.exp(-jnp.abs(x)))


def _rmsnorm(x, gain):
    return x * lax.rsqrt(jnp.mean(x * x, axis=-1, keepdims=True) + EPS) * gain


def _tile_rows(t, tm, base=0):
    return base + t * tm + lax.broadcasted_iota(jnp.int32, (tm, 1), 0)


def _adaln(x, gain, shift, scale):
    return (_rmsnorm(x, gain) * (1.0 + scale) + shift).astype(BF16)


def _mod_rows(mod_ref, j, b, row, ctx_len):
    is_ctx = row < ctx_len
    n_b = mod_ref.shape[1] - 1
    out = []
    for q in range(3):
        lat = mod_ref[3 * j + q, pl.ds(b, 1), :]
        ctx = mod_ref[3 * j + q, n_b:n_b + 1, :]
        out.append(jnp.where(is_ctx, ctx, lat))
    return out


def _mod_kernel(c_ref, w_ref, b_ref, o_ref):
    h = _silu(c_ref[...]).astype(BF16)
    o_ref[...] = _dot(h, w_ref[...].astype(BF16)) + b_ref[...]


def _modulation(cc, mod_w, mod_b):
    depth, d, _ = mod_w.shape
    rows = cc.shape[0]
    return pl.pallas_call(
        _mod_kernel,
        grid=(depth, N_MOD),
        in_specs=[
            pl.BlockSpec((rows, d), lambda l, j: (0, 0)),
            pl.BlockSpec((None, d, d), lambda l, j: (l, 0, j)),
            pl.BlockSpec((None, None, 1, d), lambda l, j: (l, j, 0, 0)),
        ],
        out_specs=pl.BlockSpec((None, None, rows, d), lambda l, j: (l, j, 0, 0)),
        out_shape=jax.ShapeDtypeStruct((depth, N_MOD, rows, d), F32),
        compiler_params=pltpu.CompilerParams(dimension_semantics=("arbitrary", "arbitrary")),
        name="modulation",
    )(cc, mod_w, mod_b.reshape(depth, N_MOD, 1, d))


def _swiglu(h_ref, w13_ref, w2_ref, acc_ref):
    for k in range(D_FF // FFN_FK):
        lo, hi = k * FFN_FK, (k + 1) * FFN_FK
        a = _dot(h_ref[...], w13_ref[:, lo:hi])
        g = _dot(h_ref[...], w13_ref[:, D_FF + lo:D_FF + hi])
        contrib = _dot((_silu(a) * g).astype(BF16), w2_ref[lo:hi, :])
        if k == 0:
            acc_ref[...] = contrib
        else:
            acc_ref[...] += contrib


def _ffn_kernel(x_ref, mod_ref, g_ref, w13_ref, w2_ref, o_ref, h_ref, acc_ref, *, j, tm, ctx_len):
    b, t = pl.program_id(0), pl.program_id(1)
    x = x_ref[...]
    shift, scale, gate = _mod_rows(mod_ref, j, b, _tile_rows(t, tm), ctx_len)
    h_ref[...] = _adaln(x, g_ref[...], shift, scale)
    _swiglu(h_ref, w13_ref, w2_ref, acc_ref)
    o_ref[...] = x + (0.5 * gate) * acc_ref[...]


def _resident(shape):
    return pl.BlockSpec(shape, lambda *_: (0,) * len(shape), pipeline_mode=pl.Buffered(1))


def _ffn(xs, mod, gain, w13, w2, *, j, n_batch, rows_per_batch, ctx_len):
    tm = FFN_TM
    tpb = rows_per_batch // tm
    d = xs.shape[1]
    tok = pl.BlockSpec((tm, d), lambda b, t: (b * tpb + t, 0))
    return pl.pallas_call(
        functools.partial(_ffn_kernel, j=j, tm=tm, ctx_len=ctx_len),
        grid=(n_batch, tpb),
        in_specs=[tok, _resident(mod.shape), _resident(gain.shape), _resident(w13.shape), _resident(w2.shape)],
        out_specs=tok,
        out_shape=jax.ShapeDtypeStruct(xs.shape, F32),
        scratch_shapes=[pltpu.VMEM((tm, d), BF16), pltpu.VMEM((tm, d), F32)],
        compiler_params=pltpu.CompilerParams(dimension_semantics=("parallel", "parallel"),
                                             vmem_limit_bytes=VMEM_LIMIT_BYTES),
        name=f"ffn{j}",
    )(xs, mod, gain, w13, w2)


def _headnorm_rope(z, gain, cos, sin):
    lane = lax.broadcasted_iota(jnp.int32, z.shape, 1)
    lo = lane < A_HEAD_DIM
    sq = z * z
    s_lo = jnp.sum(jnp.where(lo, sq, 0.0), axis=-1, keepdims=True)
    s_hi = jnp.sum(jnp.where(lo, 0.0, sq), axis=-1, keepdims=True)
    inv = lax.rsqrt(jnp.where(lo, s_lo, s_hi) * (1.0 / A_HEAD_DIM) + EPS)
    y = z * inv * gain
    first_half = (lane & (A_HEAD_DIM // 2)) == 0
    partner = jnp.where(first_half, pltpu.roll(y, LANES - A_HEAD_DIM // 2, 1), pltpu.roll(y, A_HEAD_DIM // 2, 1))
    return y * cos + partner * sin


def _spread_kv(z):
    lane = lax.broadcasted_iota(jnp.int32, z.shape, 1)
    lo = lane < A_HEAD_DIM
    a0 = jnp.where(lo, z, 0.0)
    b1 = jnp.where(lo, 0.0, z)
    return [a0, pltpu.roll(a0, A_HEAD_DIM, 1), pltpu.roll(b1, A_HEAD_DIM, 1), b1]


def _inproj_kernel(x_ref, xp_ref, xn_ref, mod_ref, g_ref, w_ref, wg_ref, bg_ref, qn_ref, kn_ref, cos_ref, sin_ref,
                   sb_ref, cw_ref, cb_ref,
                   q_o, k4_o, v4_o, gq_o, gk_o, gv_o, gr_o, mq_o, mk_o, mv_o, mo_o, gates_o, small_o, la_o,
                   h_ref, e_ref, *, tm, rows_per_batch, ctx_len):
    b, t = pl.program_id(0), pl.program_id(1)
    row = _tile_rows(t, tm)
    shift, scale, _ = _mod_rows(mod_ref, 1, b, row, ctx_len)
    h_ref[...] = _adaln(x_ref[...], g_ref[...], shift, scale)

    def proj(name, lo, hi):
        base = _OFF[name][0]
        return _dot(h_ref[...], w_ref[:, base + lo:base + hi])

    halo = xp_ref.shape[0]
    row_h = jnp.concatenate([_tile_rows(t, tm, -halo)[:halo], _tile_rows(t + 1, tm)[:halo]], axis=0)
    shift_h, scale_h, _ = _mod_rows(mod_ref, 1, b, row_h, ctx_len)
    h_halo = _adaln(jnp.concatenate([xp_ref[...], xn_ref[...]], axis=0), g_ref[...], shift_h, scale_h)
    base = _OFF["mqk"][0]
    z_halo = _dot(h_halo, w_ref[:, base:base + 2 * M_W])
    e_ref[0:halo, :] = z_halo[:halo]
    e_ref[halo + tm:, :] = z_halo[halo:]
    e_ref[halo:halo + tm, :M_W] = proj("mqk", 0, M_W)
    e_ref[halo:halo + tm, M_W:] = proj("mqk", M_W, 2 * M_W)
    pad = M_CONV // 2
    valid = []
    for tap in range(M_CONV):
        nb = row + (tap - pad)
        valid.append((nb >= 0) & (nb < rows_per_batch) & ((row < ctx_len) == (nb < ctx_len)))

    def conv_chunk(c):
        sl = slice(c * LANES, (c + 1) * LANES)
        acc = None
        for tap in range(M_CONV):
            term = jnp.where(valid[tap], e_ref[pl.ds(halo + tap - pad, tm), sl], 0.0) * cw_ref[tap:tap + 1, sl]
            acc = term if acc is None else acc + term
        y = _silu(acc + cb_ref[:, sl])
        if c < M_W // LANES:
            mq_o[:, sl] = y.astype(BF16)
        else:
            mk_o[:, c * LANES - M_W:(c + 1) * LANES - M_W] = (y * (M_HEAD_DIM ** -0.5)).astype(BF16)

    cos, sin = cos_ref[...], sin_ref[...]

    def group_aq():
        z = proj("aq", 0, A_Q)
        for s in range(A_Q // LANES):
            y = _headnorm_rope(z[:, s * LANES:(s + 1) * LANES], qn_ref[...], cos, sin)
            q_o[:, s * LANES:(s + 1) * LANES] = (y * (A_HEAD_DIM ** -0.5)).astype(BF16)

    def group_akv():
        z = proj("akv", 0, 2 * A_KV)
        for s, part in enumerate(_spread_kv(_headnorm_rope(z[:, :A_KV], kn_ref[...], cos, sin))):
            k4_o[:, s * LANES:(s + 1) * LANES] = part.astype(BF16)
        for s, part in enumerate(_spread_kv(z[:, A_KV:])):
            v4_o[:, s * LANES:(s + 1) * LANES] = part.astype(BF16)

    def group_gqk():
        z = proj("gqk", 0, 2 * G_QK)
        gq_o[...] = (z[:, :G_QK] * (G_DK ** -0.5)).astype(BF16)
        gk_o[...] = z[:, G_QK:].astype(BF16)

    def group_gv():
        gv_o[...] = proj("gv", 0, G_V).astype(BF16)

    def group_gr():
        gr_o[...] = _silu(proj("gr", 0, G_V)).astype(BF16)

    def group_mv():
        mv_o[...] = proj("mv", 0, M_W).astype(BF16)

    def group_mo():
        mo_o[...] = jax.nn.sigmoid(proj("mo", 0, M_W)).astype(BF16)

    def group_gates(s):
        gates_o[:, s * 512:(s + 1) * 512] = jax.nn.sigmoid(proj("gates", s * 512, (s + 1) * 512)).astype(BF16)

    def group_small():
        zs = proj("small", 0, LANES)
        lane = lax.broadcasted_iota(jnp.int32, zs.shape, 1)
        zb = zs + sb_ref[...]
        is_f = (lane >= SMALL_MF) & (lane < SMALL_MF + 2 * M_HEADS)
        small_o[...] = jnp.where(is_f, _log_sigmoid(zb), zb)
        la_o[...] = _log_sigmoid(_dot(zs.astype(BF16), wg_ref[...]) + bg_ref[...]) * (1.0 / G_TAU)

    groups = [group_aq, group_akv, group_gqk, group_gv, group_gr, group_mv, group_mo]
    groups += [functools.partial(group_gates, s) for s in range(3 * D_MODEL // 512)] + [group_small]
    n_conv = 2 * M_W // LANES
    for i, group in enumerate(groups):
        group()
        if i < n_conv:
            conv_chunk(i)


def _inproj(xs, mod, gain, w_in, wg, bg, qn, kn, cos, sin, sb, conv_w, conv_b, *, n_batch, rows_per_batch, ctx_len):
    tm = INPROJ_TM
    tpb = rows_per_batch // tm
    n_tok, d = xs.shape
    halo = F32_SUBLANES
    tiles_h, n_h = tm // halo, n_tok // halo

    def tok(width):
        return pl.BlockSpec((tm, width), lambda b, t: (b * tpb + t, 0))

    before = pl.BlockSpec((halo, d), lambda b, t: (jnp.maximum((b * tpb + t) * tiles_h - 1, 0), 0))
    after = pl.BlockSpec((halo, d), lambda b, t: (jnp.minimum((b * tpb + t + 1) * tiles_h, n_h - 1), 0))
    pos = pl.BlockSpec((tm, LANES), lambda b, t: (t, 0))
    widths = (A_Q, 4 * LANES, 4 * LANES, G_QK, G_QK, G_V, G_V, M_W, M_W, M_W, M_W, 3 * D_MODEL)
    out_shape = [jax.ShapeDtypeStruct((n_tok, w), BF16) for w in widths]
    out_shape += [jax.ShapeDtypeStruct((n_tok, LANES), F32), jax.ShapeDtypeStruct((n_tok, 2 * G_QK), F32)]
    out_specs = [tok(w) for w in widths] + [tok(LANES), tok(2 * G_QK)]
    consts = (mod, gain, w_in, wg, bg, qn, kn)
    return pl.pallas_call(
        functools.partial(_inproj_kernel, tm=tm, rows_per_batch=rows_per_batch, ctx_len=ctx_len),
        grid=(n_batch, tpb),
        in_specs=([tok(d), before, after] + [_resident(a.shape) for a in consts]
                  + [pos, pos, _resident(sb.shape), _resident(conv_w.shape), _resident(conv_b.shape)]),
        out_specs=out_specs,
        out_shape=out_shape,
        scratch_shapes=[pltpu.VMEM((tm, d), BF16), pltpu.VMEM((tm + 2 * halo, 2 * M_W), F32)],
        compiler_params=pltpu.CompilerParams(dimension_semantics=("parallel", "parallel"),
                                             vmem_limit_bytes=VMEM_LIMIT_BYTES),
        name="inproj",
    )(xs, xs, xs, *consts, cos, sin, sb, conv_w, conv_b)


def _attn_kernel(sink_ref, q_ref, kp_ref, kc_ref, kn_ref, kx_ref, vp_ref, vc_ref, vn_ref, vx_ref, o_ref,
                 *, n_ctx_blk, n_blk):
    j = pl.program_id(1)
    blk = A_BLOCK
    off = jnp.int32(4 * blk)
    cur_off = jnp.where(j >= n_ctx_blk, 0, off)
    prev_off = jnp.where(j >= n_ctx_blk + 1, 0, off)
    next_off = jnp.where((j >= n_ctx_blk) & (j <= n_blk - 2), 0, off)
    r = lax.broadcasted_iota(jnp.int32, (blk, blk), 0)
    c = lax.broadcasted_iota(jnp.int32, (blk, blk), 1)
    neg = -jnp.inf
    n_ctx = kx_ref.shape[0]
    bias = jnp.concatenate([
        jnp.where(c >= r + prev_off, 0.0, neg),
        jnp.where(c >= cur_off, 0.0, neg),
        jnp.where(c <= r - next_off, 0.0, neg),
        jnp.zeros((blk, n_ctx), F32)], axis=1)
    bias = jnp.concatenate([bias, bias], axis=0)
    top = lax.broadcasted_iota(jnp.int32, (2 * blk, 1), 0) < blk
    k_all = jnp.concatenate([kp_ref[...], kc_ref[...], kn_ref[...], kx_ref[...]], axis=0)
    v_all = jnp.concatenate([vp_ref[...], vc_ref[...], vn_ref[...], vx_ref[...]], axis=0)
    heads_per_kv = A_HEADS // A_KV_HEADS
    combos = [(g, half) for g in range(A_KV_HEADS) for half in range(2)]
    p, inv = {}, {}
    for g, half in combos:
        c0 = g * 2 * LANES
        qg = jnp.concatenate([q_ref[:, c0:c0 + LANES], q_ref[:, c0 + LANES:c0 + 2 * LANES]], axis=0)
        col = (2 * g + half) * LANES
        s = _dot_nt(qg, k_all[:, col:col + LANES]) + bias
        sink = jnp.where(top, sink_ref[heads_per_kv * g + half], sink_ref[heads_per_kv * g + 2 + half])
        m = jnp.maximum(jnp.max(s, axis=-1, keepdims=True), sink)
        e = jnp.exp(s - m)
        inv[g, half] = 1.0 / (jnp.sum(e, axis=-1, keepdims=True) + jnp.exp(sink - m))
        p[g, half] = e.astype(BF16)
    for g in range(A_KV_HEADS):
        c0 = g * 2 * LANES
        acc = None
        for half in range(2):
            col = (2 * g + half) * LANES
            o = _dot(p[g, half], v_all[:, col:col + LANES]) * inv[g, half]
            acc = o if acc is None else acc + o
        o_ref[:, c0:c0 + LANES] = acc[:blk].astype(BF16)
        o_ref[:, c0 + LANES:c0 + 2 * LANES] = acc[blk:].astype(BF16)


def _attention(sink, q, k4, v4, *, n_batch, rows_per_batch, ctx_len):
    blk = A_BLOCK
    n_blk = rows_per_batch // blk
    n_ctx_blk = ctx_len // blk
    n_tok, width = q.shape
    cur = lambda b, j: (b * n_blk + j, 0)
    prev = lambda b, j: (b * n_blk + jnp.maximum(j - 1, 0), 0)
    nxt = lambda b, j: (b * n_blk + jnp.minimum(j + 1, n_blk - 1), 0)
    ctx = lambda b, j: (b * (rows_per_batch // ctx_len), 0)
    kv_specs = [pl.BlockSpec((blk, width), prev), pl.BlockSpec((blk, width), cur),
                pl.BlockSpec((blk, width), nxt), pl.BlockSpec((ctx_len, width), ctx)]
    return pl.pallas_call(
        functools.partial(_attn_kernel, n_ctx_blk=n_ctx_blk, n_blk=n_blk),
        grid=(n_batch, n_blk),
        in_specs=[pl.BlockSpec(memory_space=pltpu.SMEM), pl.BlockSpec((blk, width), cur)] + kv_specs + kv_specs,
        out_specs=pl.BlockSpec((blk, width), cur),
        out_shape=jax.ShapeDtypeStruct((n_tok, width), BF16),
        compiler_params=pltpu.CompilerParams(dimension_semantics=("parallel", "parallel"),
                                             vmem_limit_bytes=VMEM_LIMIT_BYTES),
        name="attention",
    )(sink, q, k4, k4, k4, k4, v4, v4, v4, v4)


def _gla_direction(q_ref, k_ref, v_ref, la_ref, o_ref, st_ref, *, reverse):
    n = q_ref.shape[0]
    r = lax.broadcasted_iota(jnp.int32, (n, n), 0)
    c = lax.broadcasted_iota(jnp.int32, (n, n), 1)
    keep = (c >= r) if reverse else (c <= r)
    last = 0 if reverse else n - 1
    tri = jnp.where(keep, 1.0, 0.0).astype(BF16)
    cum = _cumdot(tri, la_ref[...])
    mid = cum[n // 2:n // 2 + 1, :]
    end = cum[last:last + 1, :]
    qf, kf = q_ref[...].astype(F32), k_ref[...].astype(F32)
    q_in = (qf * jnp.exp(cum - mid)).astype(BF16)
    k_in = (kf * jnp.exp(mid - cum)).astype(BF16)
    q_st = (qf * jnp.exp(cum)).astype(BF16)
    k_st = (kf * jnp.exp(end - cum)).astype(BF16)
    decay = jnp.exp(end)
    lane = lax.broadcasted_iota(jnp.int32, (n, LANES), 1)
    lo = lane < G_DK
    zero = jnp.zeros((), BF16)
    rr = lax.broadcasted_iota(jnp.int32, (2 * G_DV, 2 * G_DK), 0)
    cc = lax.broadcasted_iota(jnp.int32, (2 * G_DV, 2 * G_DK), 1)
    diag = (rr < G_DV) == (cc < G_DK)
    slabs = range(G_HEADS // 2)
    sls = [slice(s * LANES, (s + 1) * LANES) for s in slabs]
    yield
    a = []
    for s in slabs:
        qs, ks = q_in[:, sls[s]], k_in[:, sls[s]]
        a.append((jnp.where(keep, _dot_nt(qs, jnp.where(lo, ks, zero)), 0.0).astype(BF16),
                  jnp.where(keep, _dot_nt(qs, jnp.where(lo, zero, ks)), 0.0).astype(BF16)))
    yield
    for s in slabs:
        inter = _dot_nt(q_st[:, sls[s]], st_ref[s].astype(BF16))
        for half in range(2):
            vsl = slice((2 * s + half) * G_DV, (2 * s + half + 1) * G_DV)
            o_ref[:, vsl] = (_dot(a[s][half], v_ref[:, vsl])
                             + inter[:, half * G_DV:(half + 1) * G_DV]).astype(o_ref.dtype)
    yield
    for s in slabs:
        upd = _dot_tn(v_ref[:, 2 * s * G_DV:(2 * s + 2) * G_DV], k_st[:, sls[s]])
        st_ref[s] = decay[:, sls[s]] * st_ref[s] + jnp.where(diag, upd, 0.0)
    yield


def _run_staged(generators):
    for _ in zip(*generators):
        pass


def _gla_kernel(qf_ref, kf_ref, vf_ref, laf_ref, qb_ref, kb_ref, vb_ref, lab_ref, of_ref, ob_ref, sf_ref, sb_ref):
    @pl.when(pl.program_id(1) == 0)
    def _():
        sf_ref[...] = jnp.zeros_like(sf_ref)
        sb_ref[...] = jnp.zeros_like(sb_ref)

    chains = []
    for g in range(qf_ref.shape[0]):
        chains.append(_gla_direction(qf_ref.at[g], kf_ref.at[g], vf_ref.at[g], laf_ref.at[g], of_ref.at[g],
                                     sf_ref.at[g], reverse=False))
        chains.append(_gla_direction(qb_ref.at[g], kb_ref.at[g], vb_ref.at[g], lab_ref.at[g], ob_ref.at[g],
                                     sb_ref.at[g], reverse=True))
    _run_staged(chains)


def _scan_maps(n_chunks, n_ctx_chunks):
    def fwd(i):
        return i

    def bwd(i):
        return jnp.where(i < n_ctx_chunks, n_ctx_chunks - 1 - i, n_chunks - 1 + n_ctx_chunks - i)

    return fwd, bwd


def _per_sample(a, n_batch):
    return a.reshape(n_batch, a.shape[0] // n_batch, a.shape[1])


def _gla(gq, gk, gv, la, *, n_batch, rows_per_batch, ctx_len):
    n, grp = GLA_CHUNK, SCAN_BATCH_GROUP
    n_chunks = rows_per_batch // n
    fwd, bwd = _scan_maps(n_chunks, ctx_len // n)
    n_tok = gq.shape[0]
    gq, gk, gv, la = (_per_sample(a, n_batch) for a in (gq, gk, gv, la))

    def specs(chunk, la_col):
        return [pl.BlockSpec((grp, n, G_QK), lambda b, i: (b, chunk(i), 0)),
                pl.BlockSpec((grp, n, G_QK), lambda b, i: (b, chunk(i), 0)),
                pl.BlockSpec((grp, n, G_V), lambda b, i: (b, chunk(i), 0)),
                pl.BlockSpec((grp, n, G_QK), lambda b, i: (b, chunk(i), la_col))]

    state = pltpu.VMEM((grp, G_HEADS // 2, 2 * G_DV, 2 * G_DK), F32)
    out_f, out_b = pl.pallas_call(
        _gla_kernel,
        grid=(n_batch // grp, n_chunks),
        in_specs=specs(fwd, 0) + specs(bwd, 1),
        out_specs=[pl.BlockSpec((grp, n, G_V), lambda b, i: (b, fwd(i), 0)),
                   pl.BlockSpec((grp, n, G_V), lambda b, i: (b, bwd(i), 0))],
        out_shape=[jax.ShapeDtypeStruct((n_batch, rows_per_batch, G_V), BF16)] * 2,
        scratch_shapes=[state, state],
        compiler_params=pltpu.CompilerParams(dimension_semantics=("arbitrary", "arbitrary"),
                                             vmem_limit_bytes=VMEM_LIMIT_BYTES),
        name="gla",
    )(gq, gk, gv, la, gq, gk, gv, la)
    return out_f.reshape(n_tok, G_V), out_b.reshape(n_tok, G_V)


def _cumdot_right(x, tri):
    hi = x.astype(BF16)
    r1 = x - hi.astype(F32)
    mid = r1.astype(BF16)
    lo = (r1 - mid.astype(F32)).astype(BF16)
    return _dot(hi, tri) + _dot(mid, tri) + _dot(lo, tri)


def _mlstm_direction(q_ref, k_ref, v_ref, small_ref, o_ref, ct_ref, n_ref, m_ref, *, direction):
    n = q_ref.shape[0]
    reverse = direction == 1
    n_gates = 2 * M_HEADS
    jj = lax.broadcasted_iota(jnp.int32, (n, n), 0)
    ii = lax.broadcasted_iota(jnp.int32, (n, n), 1)
    vis = (jj >= ii) if reverse else (jj <= ii)
    last = 0 if reverse else n - 1
    vis_bf = jnp.where(vis, 1.0, 0.0).astype(BF16)
    gates_t = small_ref[...].T[SMALL_MI:SMALL_MI + 2 * n_gates, :]
    b_rows = _cumdot_right(gates_t, vis_bf)[n_gates:, :]
    u_rows = gates_t[:n_gates, :] - b_rows
    m_prev = m_ref[direction][:, 0:1]
    lane = lax.broadcasted_iota(jnp.int32, (n_gates, n), 1)
    m_rel = u_rows
    for step in range(n.bit_length() - 1):
        s = 1 << step
        if reverse:
            shifted = jnp.where(lane < n - s, pltpu.roll(m_rel, n - s, 1), -jnp.inf)
        else:
            shifted = jnp.where(lane >= s, pltpu.roll(m_rel, s, 1), -jnp.inf)
        m_rel = jnp.maximum(m_rel, shifted)
    m_rel = jnp.maximum(m_rel, m_prev)
    w_inter = jnp.exp(m_prev - m_rel)
    m_rows = b_rows + m_rel
    inv_floor = jnp.exp(-m_rows)
    m_new = m_rows[:, last:last + 1]
    b_end = b_rows[:, last:last + 1]
    decay = jnp.exp(b_end + m_prev - m_new)
    m_ref[direction] = jnp.broadcast_to(m_new, m_ref.shape[1:])
    log2e = 1.4426950408889634
    m2_rows = m_rel * log2e
    c2 = (b_end - m_new) * log2e
    u2_cols = jnp.concatenate([u_rows * log2e, jnp.zeros((LANES - n_gates, n), F32)], axis=0).T
    v_t = v_ref[...].T
    ones = jnp.ones((8, n), BF16)
    heads = range(M_HEADS)
    sls = [slice(h * M_HEAD_DIM, (h + 1) * M_HEAD_DIM) for h in heads]
    rows = [slice(direction * M_HEADS + h, direction * M_HEADS + h + 1) for h in heads]
    u2, s_t = [], []
    for h in heads:
        u2.append(jnp.broadcast_to(u2_cols[:, rows[h]], (n, n)))
        w_t = jnp.exp2(jnp.where(vis, u2[h] - m2_rows[rows[h], :], -jnp.inf))
        s_t.append((_dot_nt(k_ref[:, sls[h]], q_ref[:, sls[h]]) * w_t).astype(BF16))
    yield
    for h in heads:
        idx = direction * M_HEADS + h
        qh = q_ref[:, sls[h]]
        nq = _dot_nt(n_ref[idx].astype(BF16), qh)[0:1, :]
        den = w_inter[rows[h], :] * nq + _dot(ones, s_t[h])[0:1, :]
        norm = 1.0 / jnp.maximum(jnp.abs(den), inv_floor[rows[h], :])
        h_t = (_dot(v_t[sls[h], :], s_t[h])
               + _dot_nt(ct_ref[idx].astype(BF16), qh) * w_inter[rows[h], :]) * norm
        o_ref[:, sls[h]] = h_t.astype(o_ref.dtype).T
    yield
    for h in heads:
        idx = direction * M_HEADS + h
        kw = k_ref[:, sls[h]].astype(F32) * jnp.exp2(u2[h][:, :M_HEAD_DIM] + c2[rows[h], :])
        ct_ref[idx] = decay[rows[h], :] * ct_ref[idx] + _dot(v_t[sls[h], :], kw.astype(BF16))
        n_new = decay[rows[h], :] * n_ref[idx][0:1, :] + jnp.sum(kw, axis=0, keepdims=True)
        n_ref[idx] = jnp.broadcast_to(n_new, n_ref.shape[1:])
    yield


def _mlstm_kernel(qf_ref, kf_ref, vf_ref, sf_ref, qb_ref, kb_ref, vb_ref, sb_ref, of_ref, ob_ref,
                  c_ref, n_ref, m_ref):
    @pl.when(pl.program_id(1) == 0)
    def _():
        c_ref[...] = jnp.zeros_like(c_ref)
        n_ref[...] = jnp.zeros_like(n_ref)
        m_ref[...] = jnp.zeros_like(m_ref)

    chains = []
    for g in range(qf_ref.shape[0]):
        state = (c_ref.at[g], n_ref.at[g], m_ref.at[g])
        chains.append(_mlstm_direction(qf_ref.at[g], kf_ref.at[g], vf_ref.at[g], sf_ref.at[g], of_ref.at[g],
                                       *state, direction=0))
        chains.append(_mlstm_direction(qb_ref.at[g], kb_ref.at[g], vb_ref.at[g], sb_ref.at[g], ob_ref.at[g],
                                       *state, direction=1))
    _run_staged(chains)


def _mlstm(mq, mk, mv, small, *, n_batch, rows_per_batch, ctx_len):
    n, grp = MLSTM_CHUNK, SCAN_BATCH_GROUP
    n_chunks = rows_per_batch // n
    fwd, bwd = _scan_maps(n_chunks, ctx_len // n)
    n_tok = mq.shape[0]
    mq, mk, mv, small = (_per_sample(a, n_batch) for a in (mq, mk, mv, small))

    def specs(chunk):
        return ([pl.BlockSpec((grp, n, M_W), lambda b, i: (b, chunk(i), 0))] * 3
                + [pl.BlockSpec((grp, n, LANES), lambda b, i: (b, chunk(i), 0))])

    out_f, out_b = pl.pallas_call(
        _mlstm_kernel,
        grid=(n_batch // grp, n_chunks),
        in_specs=specs(fwd) + specs(bwd),
        out_specs=[pl.BlockSpec((grp, n, M_W), lambda b, i: (b, fwd(i), 0)),
                   pl.BlockSpec((grp, n, M_W), lambda b, i: (b, bwd(i), 0))],
        out_shape=[jax.ShapeDtypeStruct((n_batch, rows_per_batch, M_W), BF16)] * 2,
        scratch_shapes=[pltpu.VMEM((grp, 2 * M_HEADS, M_HEAD_DIM, M_HEAD_DIM), F32),
                        pltpu.VMEM((grp, 2 * M_HEADS, 8, M_HEAD_DIM), F32),
                        pltpu.VMEM((grp, 2, 2 * M_HEADS, LANES), F32)],
        compiler_params=pltpu.CompilerParams(dimension_semantics=("arbitrary", "arbitrary"),
                                             vmem_limit_bytes=VMEM_LIMIT_BYTES),
        name="mlstm",
    )(mq, mk, mv, small, mq, mk, mv, small)
    return out_f.reshape(n_tok, M_W), out_b.reshape(n_tok, M_W)


def _headnorm128(x, gain):
    parts = [_rmsnorm(x[:, s * LANES:(s + 1) * LANES], gain) for s in range(x.shape[1] // LANES)]
    return jnp.concatenate(parts, axis=1)


def _merge_ffn_kernel(x_ref, att_ref, gf_ref, gb_ref, hf_ref, hb_ref, gr_ref, mo_ref, gates_ref,
                      mod_ref, gn_ref, mn_ref, wa_ref, wg_ref, wm_ref, wo_ref, g_ref, w13_ref, w2_ref,
                      o_ref, h_ref, acc_ref, *, tm, ctx_len, row_base):
    b, t = pl.program_id(0), pl.program_id(1)
    d = x_ref.shape[1]
    row = _tile_rows(t, tm, row_base)
    gla = gf_ref[...].astype(F32) + gb_ref[...].astype(F32)
    g = (_headnorm128(gla, gn_ref[...]) * gr_ref[...].astype(F32)).astype(BF16)
    mls = hf_ref[...].astype(F32) + hb_ref[...].astype(F32)
    m = (_headnorm128(mls, mn_ref[...]) * mo_ref[...].astype(F32)).astype(BF16)
    y = (gates_ref[:, 0:d].astype(F32) * _dot(att_ref[...], wa_ref[...])
         + gates_ref[:, d:2 * d].astype(F32) * _dot(g, wg_ref[...])
         + gates_ref[:, 2 * d:3 * d].astype(F32) * _dot(m, wm_ref[...]))
    _, _, gate = _mod_rows(mod_ref, 1, b, row, ctx_len)
    x = x_ref[...] + gate * _dot(y.astype(BF16), wo_ref[...])
    shift, scale, gate = _mod_rows(mod_ref, 2, b, row, ctx_len)
    h_ref[...] = _adaln(x, g_ref[...], shift, scale)
    _swiglu(h_ref, w13_ref, w2_ref, acc_ref)
    o_ref[...] = x + (0.5 * gate) * acc_ref[...]


def _merge_ffn(streams, consts, *, n_batch, rows_per_batch, ctx_len, latent_only):
    xs = streams[0]
    d = xs.shape[1]
    if latent_only:
        tm = MERGE_TM_LATENT
        n_rows = rows_per_batch - ctx_len
        tpb = n_rows // tm

        def tok(width):
            align = math.gcd(rows_per_batch, ctx_len, tm)
            return pl.BlockSpec((pl.Element(tm), pl.Element(width)),
                                lambda b, t: (pl.multiple_of(b * rows_per_batch + ctx_len + t * tm, align), 0))
    else:
        tm = INPROJ_TM
        n_rows = rows_per_batch
        tpb = n_rows // tm

        def tok(width):
            return pl.BlockSpec((tm, width), lambda b, t: (b * tpb + t, 0))

    return pl.pallas_call(
        functools.partial(_merge_ffn_kernel, tm=tm, ctx_len=ctx_len, row_base=ctx_len if latent_only else 0),
        grid=(n_batch, tpb),
        in_specs=[tok(a.shape[1]) for a in streams] + [_resident(a.shape) for a in consts],
        out_specs=pl.BlockSpec((tm, d), lambda b, t: (b * tpb + t, 0)),
        out_shape=jax.ShapeDtypeStruct((n_batch * n_rows, d), F32),
        scratch_shapes=[pltpu.VMEM((tm, d), BF16), pltpu.VMEM((tm, d), F32)],
        compiler_params=pltpu.CompilerParams(dimension_semantics=("parallel", "parallel"),
                                             vmem_limit_bytes=VMEM_LIMIT_BYTES),
        name="merge_ffn",
    )(*streams, *consts)


def _rope_tables(seq, ctx_len):
    rows = seq // GRID_W
    r = jnp.repeat(jnp.arange(rows), GRID_W).astype(F32)
    col = jnp.tile(jnp.arange(GRID_W), rows).astype(F32)
    n_freq = A_HEAD_DIM // 4
    inv = ROPE_BASE ** (-jnp.arange(n_freq, dtype=F32) / n_freq)
    ang = jnp.concatenate([r[:, None] * inv, col[:, None] * inv], axis=-1)
    cos, sin = jnp.cos(ang), jnp.sin(ang)
    cos = jnp.concatenate([cos, cos, cos, cos], axis=-1)
    sin = jnp.concatenate([-sin, sin, -sin, sin], axis=-1)
    cos = jnp.concatenate([jnp.ones((ctx_len, LANES), F32), cos], axis=0)
    sin = jnp.concatenate([jnp.zeros((ctx_len, LANES), F32), sin], axis=0)
    return cos, sin


def _reorder_w_in(w):
    idx = np.cumsum((0,) + IN_SPLITS)
    (aq, ak, av, gq, gk, gv, gr, gg, mq, mk, mv, mo, mi, mf, s_a, s_g, s_m) = [
        w[:, int(idx[i]):int(idx[i + 1])] for i in range(len(IN_SPLITS))]
    pad = jnp.zeros((w.shape[0], LANES - gg.shape[1] - mi.shape[1] - mf.shape[1]), w.dtype)
    return jnp.concatenate([aq, ak, av, gq, gk, gv, gr, mq, mk, mv, mo, s_a, s_g, s_m, gg, mi, mf, pad], axis=1)


def kernel(x, c, ctx, c_ctx, mod_w, mod_b, norm_g, ffn1_w13, ffn1_w2, ffn2_w13, ffn2_w2, w_in, attn_q_norm,
           attn_k_norm, attn_sink, gla_w2, gla_b, gla_norm, mlstm_conv_w, mlstm_conv_b, mlstm_ib, mlstm_fb,
           mlstm_norm, w_out_attn, w_out_gla, w_out_mlstm, w_o):
    n_batch, seq, d = x.shape
    ctx_len = ctx.shape[1]
    depth = mod_w.shape[0]
    rows_per_batch = ctx_len + seq
    assert d == D_MODEL and n_batch < MOD_ROWS
    assert rows_per_batch % FFN_TM == 0 and rows_per_batch % INPROJ_TM == 0
    assert ctx_len % MLSTM_CHUNK == 0 and seq % MLSTM_CHUNK == 0 and seq % GRID_W == 0
    assert seq % MERGE_TM_LATENT == 0 and n_batch % SCAN_BATCH_GROUP == 0
    geo = dict(n_batch=n_batch, rows_per_batch=rows_per_batch, ctx_len=ctx_len)

    xs = jnp.concatenate([ctx, x], axis=1).reshape(n_batch * rows_per_batch, d)
    cc = jnp.concatenate([c, jnp.broadcast_to(c_ctx[None], (MOD_ROWS - n_batch, d))], axis=0)
    mod_all = _modulation(cc, mod_w, mod_b)[:, :, :n_batch + 1]
    cos, sin = _rope_tables(seq, ctx_len)

    for l in range(depth):
        mod = mod_all[l]
        gains = norm_g[l]
        xs = _ffn(xs, mod, gains[0:1], ffn1_w13[l].astype(BF16), ffn1_w2[l].astype(BF16), j=0, **geo)

        wg = jnp.zeros((LANES, 2 * G_QK), F32)
        wg = wg.at[0:G_RANK, 0:G_QK].set(gla_w2[l, 0]).at[G_RANK:2 * G_RANK, G_QK:].set(gla_w2[l, 1])
        sb = jnp.zeros((1, LANES), F32)
        sb = sb.at[0, SMALL_MI:SMALL_MI + 2 * M_HEADS].set(mlstm_ib[l].reshape(-1))
        sb = sb.at[0, SMALL_MF:SMALL_MF + 2 * M_HEADS].set(mlstm_fb[l].reshape(-1))
        (q, k4, v4, gq, gk, gv, gr, mq, mk, mv, mo, gates, small, la) = _inproj(
            xs, mod, gains[1:2], _reorder_w_in(w_in[l]).astype(BF16), wg.astype(BF16),
            gla_b[l].reshape(1, 2 * G_QK), jnp.tile(attn_q_norm[l], 2)[None], jnp.tile(attn_k_norm[l], 2)[None],
            cos, sin, sb, mlstm_conv_w[l], mlstm_conv_b[l][None], **geo)

        att = _attention(attn_sink[l], q, k4, v4, **geo)
        gf, gb = _gla(gq, gk, gv, la, **geo)
        hf, hb = _mlstm(mq, mk, mv, small, **geo)
        consts = (mod, gla_norm[l][None], mlstm_norm[l][None], w_out_attn[l].astype(BF16),
                  w_out_gla[l].astype(BF16), w_out_mlstm[l].astype(BF16), w_o[l].astype(BF16),
                  gains[2:3], ffn2_w13[l].astype(BF16), ffn2_w2[l].astype(BF16))
        xs = _merge_ffn((xs, att, gf, gb, hf, hb, gr, mo, gates), consts, latent_only=l == depth - 1, **geo)

    return xs.reshape(n_batch, seq, d)
```

```python
import functools
import math

import jax
import jax.numpy as jnp
import numpy as np
from jax import lax
from jax.experimental import pallas as pl
from jax.experimental.pallas import tpu as pltpu

F32 = jnp.float32
BF16 = jnp.bfloat16

D_MODEL = 1024
GRID_W = 64
A_HEADS, A_KV_HEADS, A_HEAD_DIM = 8, 2, 64
WINDOW = A_BLOCK = 128
ROPE_BASE = 10000.0
G_HEADS, G_DK, G_DV, G_RANK, G_TAU = 4, 64, 128, 16, 16.0
M_HEADS, M_HEAD_DIM, M_CONV = 4, 128, 5
D_FF = 2816
N_MOD = 9
EPS = 1e-6
A_Q = A_HEADS * A_HEAD_DIM
A_KV = A_KV_HEADS * A_HEAD_DIM
G_QK = G_HEADS * G_DK
G_V = G_HEADS * G_DV
M_W = M_HEADS * M_HEAD_DIM
IN_SPLITS = (A_Q, A_KV, A_KV, G_QK, G_QK, G_V, G_V, 2 * G_RANK,
             M_W, M_W, M_W, M_W, 2 * M_HEADS, 2 * M_HEADS, D_MODEL, D_MODEL, D_MODEL)

LANES = 128
F32_SUBLANES = 8
VMEM_LIMIT_BYTES = 56 * 1024 * 1024

FFN_TM = 768
INPROJ_TM = 384
MERGE_TM_LATENT = 512
FFN_FK = 256
GLA_CHUNK = 128
MLSTM_CHUNK = 256
SCAN_BATCH_GROUP = 4
ATTN_BATCH_GROUP = 4
MOD_ROWS = 8

SMALL_GG = 0
SMALL_MI = 2 * G_RANK
SMALL_MF = SMALL_MI + 2 * M_HEADS

_OFF = {}
_o = 0
for _name, _w in (("aq", A_Q), ("akv", 2 * A_KV), ("gqk", 2 * G_QK), ("gv", G_V), ("gr", G_V),
                  ("mqk", 2 * M_W), ("mv", M_W), ("mo", M_W), ("gates", 3 * D_MODEL), ("small", LANES)):
    _OFF[_name] = (_o, _o + _w)
    _o += _w
D_IN_PAD = _o


def _dot(a, b):
    return jnp.dot(a, b, preferred_element_type=F32)


def _dot_nt(a, b):
    return lax.dot_general(a, b, (((1,), (1,)), ((), ())), preferred_element_type=F32)


def _dot_tn(a, b):
    return lax.dot_general(a, b, (((0,), (0,)), ((), ())), preferred_element_type=F32)


def _cumdot(tri, x):
    hi = x.astype(BF16)
    r1 = x - hi.astype(F32)
    mid = r1.astype(BF16)
    lo = (r1 - mid.astype(F32)).astype(BF16)
    return _dot(tri, hi) + _dot(tri, mid) + _dot(tri, lo)


def _sigmoid(x):
    return 0.5 * jnp.tanh(0.5 * x) + 0.5


def _silu(x):
    h = 0.5 * x
    return h * jnp.tanh(h) + h


def _log_sigmoid(x):
    return jnp.minimum(x, 0.0) - jnp.log1p(jnp.exp(-jnp.abs(x)))


def _rmsnorm(x, gain):
    return x * lax.rsqrt(jnp.mean(x * x, axis=-1, keepdims=True) + EPS) * gain


def _tile_rows(t, tm, base=0):
    return base + t * tm + lax.broadcasted_iota(jnp.int32, (tm, 1), 0)


def _adaln(x, gain, shift, scale):
    return (_rmsnorm(x, gain) * (1.0 + scale) + shift).astype(BF16)


def _adaln_tile(h_ref, x, gain, mod_ref, j, b, row, ctx_len, first_row):
    def latent_only():
        shift = mod_ref[3 * j, pl.ds(b, 1), :]
        scale = mod_ref[3 * j + 1, pl.ds(b, 1), :]
        inv = lax.rsqrt(jnp.mean(x * x, axis=-1, keepdims=True) + EPS)
        h_ref[...] = (x * inv * (gain * (1.0 + scale)) + shift).astype(BF16)

    def mixed():
        shift, scale, _ = _mod_rows(mod_ref, j, b, row, ctx_len)
        h_ref[...] = _adaln(x, gain, shift, scale)

    if isinstance(first_row, int):
        latent_only() if first_row >= ctx_len else mixed()
    else:
        pl.when(first_row >= ctx_len)(latent_only)
        pl.when(first_row < ctx_len)(mixed)


def _mod_rows(mod_ref, j, b, row, ctx_len):
    is_ctx = row < ctx_len
    n_b = mod_ref.shape[1] - 1
    out = []
    for q in range(3):
        lat = mod_ref[3 * j + q, pl.ds(b, 1), :]
        ctx = mod_ref[3 * j + q, n_b:n_b + 1, :]
        out.append(jnp.where(is_ctx, ctx, lat))
    return out


def _mod_kernel(c_ref, w_ref, b_ref, o_ref):
    h = _silu(c_ref[...]).astype(BF16)
    o_ref[...] = _dot(h, w_ref[...].astype(BF16)) + b_ref[...]


def _modulation(cc, mod_w, mod_b):
    depth, d, _ = mod_w.shape
    rows = cc.shape[0]
    return pl.pallas_call(
        _mod_kernel,
        grid=(depth, N_MOD),
        in_specs=[
            pl.BlockSpec((rows, d), lambda l, j: (0, 0)),
            pl.BlockSpec((None, d, d), lambda l, j: (l, 0, j)),
            pl.BlockSpec((None, None, 1, d), lambda l, j: (l, j, 0, 0)),
        ],
        out_specs=pl.BlockSpec((None, None, rows, d), lambda l, j: (l, j, 0, 0)),
        out_shape=jax.ShapeDtypeStruct((depth, N_MOD, rows, d), F32),
        compiler_params=pltpu.CompilerParams(dimension_semantics=("arbitrary", "arbitrary")),
        name="modulation",
    )(cc, mod_w, mod_b.reshape(depth, N_MOD, 1, d))


def _swiglu(h_ref, w13_ref, w2_ref, acc_ref):
    for k in range(D_FF // FFN_FK):
        lo, hi = k * FFN_FK, (k + 1) * FFN_FK
        a = _dot(h_ref[...], w13_ref[:, lo:hi])
        g = _dot(h_ref[...], w13_ref[:, D_FF + lo:D_FF + hi])
        contrib = _dot((_silu(a) * g).astype(BF16), w2_ref[lo:hi, :])
        if k == 0:
            acc_ref[...] = contrib
        else:
            acc_ref[...] += contrib


def _ffn_kernel(x_ref, mod_ref, g_ref, w13_ref, w2_ref, o_ref, h_ref, acc_ref, *, j, tm, ctx_len):
    b, t = pl.program_id(0), pl.program_id(1)
    x = x_ref[...]
    row = _tile_rows(t, tm)
    _adaln_tile(h_ref, x, g_ref[...], mod_ref, j, b, row, ctx_len, t * tm)
    _, _, gate = _mod_rows(mod_ref, j, b, row, ctx_len)
    _swiglu(h_ref, w13_ref, w2_ref, acc_ref)
    o_ref[...] = x + (0.5 * gate) * acc_ref[...]


def _resident(shape):
    return pl.BlockSpec(shape, lambda *_: (0,) * len(shape), pipeline_mode=pl.Buffered(1))


def _ffn(xs, mod, gain, w13, w2, *, j, n_batch, rows_per_batch, ctx_len):
    tm = FFN_TM
    tpb = rows_per_batch // tm
    d = xs.shape[1]
    tok = pl.BlockSpec((tm, d), lambda b, t: (b * tpb + t, 0))
    return pl.pallas_call(
        functools.partial(_ffn_kernel, j=j, tm=tm, ctx_len=ctx_len),
        grid=(n_batch, tpb),
        in_specs=[tok, _resident(mod.shape), _resident(gain.shape), _resident(w13.shape), _resident(w2.shape)],
        out_specs=tok,
        out_shape=jax.ShapeDtypeStruct(xs.shape, F32),
        scratch_shapes=[pltpu.VMEM((tm, d), BF16), pltpu.VMEM((tm, d), F32)],
        compiler_params=pltpu.CompilerParams(dimension_semantics=("parallel", "parallel"),
                                             vmem_limit_bytes=VMEM_LIMIT_BYTES),
        name=f"ffn{j}",
    )(xs, mod, gain, w13, w2)


def _headnorm_rope(z, gain, cos, sin):
    lane = lax.broadcasted_iota(jnp.int32, z.shape, 1)
    lo = lane < A_HEAD_DIM
    sq = z * z
    s_lo = jnp.sum(jnp.where(lo, sq, 0.0), axis=-1, keepdims=True)
    s_hi = jnp.sum(jnp.where(lo, 0.0, sq), axis=-1, keepdims=True)
    inv = lax.rsqrt(jnp.where(lo, s_lo, s_hi) * (1.0 / A_HEAD_DIM) + EPS)
    y = z * inv * gain
    first_half = (lane & (A_HEAD_DIM // 2)) == 0
    partner = jnp.where(first_half, pltpu.roll(y, LANES - A_HEAD_DIM // 2, 1), pltpu.roll(y, A_HEAD_DIM // 2, 1))
    return y * cos + partner * sin


def _spread_kv(z):
    lane = lax.broadcasted_iota(jnp.int32, z.shape, 1)
    lo = lane < A_HEAD_DIM
    a0 = jnp.where(lo, z, 0.0)
    b1 = jnp.where(lo, 0.0, z)
    return [a0, pltpu.roll(a0, A_HEAD_DIM, 1), pltpu.roll(b1, A_HEAD_DIM, 1), b1]


def _ones_lane(half):
    return A_HEAD_DIM if half == 0 else 0


def _with_ones_lane(parts):
    lane = lax.broadcasted_iota(jnp.int32, parts[0].shape, 1)
    return [jnp.where(lane == _ones_lane(i % 2), 1.0, part) for i, part in enumerate(parts)]


def _inproj_kernel(x_ref, xp_ref, xn_ref, mod_ref, g_ref, w_ref, wg_ref, bg_ref, qn_ref, kn_ref, cos_ref, sin_ref,
                   sb_ref, cw_ref, cb_ref,
                   q_o, k4_o, v4_o, gq_o, gk_o, gv_o, gr_o, mq_o, mk_o, mv_o, mo_o, gates_o, small_o, la_o,
                   h_ref, e_ref, *, tm, rows_per_batch, ctx_len):
    b, t = pl.program_id(0), pl.program_id(1)
    row = _tile_rows(t, tm)
    _adaln_tile(h_ref, x_ref[...], g_ref[...], mod_ref, 1, b, row, ctx_len, t * tm)

    def proj(name, lo, hi):
        base = _OFF[name][0]
        return _dot(h_ref[...], w_ref[:, base + lo:base + hi])

    halo = xp_ref.shape[0]
    row_h = jnp.concatenate([_tile_rows(t, tm, -halo)[:halo], _tile_rows(t + 1, tm)[:halo]], axis=0)
    shift_h, scale_h, _ = _mod_rows(mod_ref, 1, b, row_h, ctx_len)
    h_halo = _adaln(jnp.concatenate([xp_ref[...], xn_ref[...]], axis=0), g_ref[...], shift_h, scale_h)
    base = _OFF["mqk"][0]
    z_halo = _dot(h_halo, w_ref[:, base:base + 2 * M_W])
    e_ref[0:halo, :] = z_halo[:halo]
    e_ref[halo + tm:, :] = z_halo[halo:]
    e_ref[halo:halo + tm, :M_W] = proj("mqk", 0, M_W)
    e_ref[halo:halo + tm, M_W:] = proj("mqk", M_W, 2 * M_W)
    pad = M_CONV // 2
    acc = None
    for tap in range(M_CONV):
        nb = row + (tap - pad)
        valid = (nb >= 0) & (nb < rows_per_batch) & ((row < ctx_len) == (nb < ctx_len))
        term = jnp.where(valid, e_ref[pl.ds(halo + tap - pad, tm), :], 0.0) * cw_ref[tap:tap + 1, :]
        acc = term if acc is None else acc + term
    y = _silu(acc + cb_ref[...])
    mq_o[...] = y[:, :M_W].astype(BF16)
    mk_o[...] = (y[:, M_W:] * (M_HEAD_DIM ** -0.5)).astype(BF16)

    cos, sin = cos_ref[...], sin_ref[...]
    z = proj("aq", 0, A_Q)
    for s in range(A_Q // LANES):
        y = _headnorm_rope(z[:, s * LANES:(s + 1) * LANES], qn_ref[...], cos, sin)
        q_o[:, s * LANES:(s + 1) * LANES] = (y * (A_HEAD_DIM ** -0.5)).astype(BF16)
    z = proj("akv", 0, 2 * A_KV)
    for s, part in enumerate(_spread_kv(_headnorm_rope(z[:, :A_KV], kn_ref[...], cos, sin))):
        k4_o[:, s * LANES:(s + 1) * LANES] = part.astype(BF16)
    for s, part in enumerate(_with_ones_lane(_spread_kv(z[:, A_KV:]))):
        v4_o[:, s * LANES:(s + 1) * LANES] = part.astype(BF16)
    z = proj("gqk", 0, 2 * G_QK)
    gq_o[...] = (z[:, :G_QK] * (G_DK ** -0.5)).astype(BF16)
    gk_o[...] = z[:, G_QK:].astype(BF16)
    gv_o[...] = proj("gv", 0, G_V).astype(BF16)
    gr_o[...] = _silu(proj("gr", 0, G_V)).astype(BF16)
    mv_o[...] = proj("mv", 0, M_W).astype(BF16)
    mo_o[...] = _sigmoid(proj("mo", 0, M_W)).astype(BF16)
    for s in range(3 * D_MODEL // 512):
        gates_o[:, s * 512:(s + 1) * 512] = _sigmoid(proj("gates", s * 512, (s + 1) * 512)).astype(BF16)
    zs = proj("small", 0, LANES)
    lane = lax.broadcasted_iota(jnp.int32, zs.shape, 1)
    zb = zs + sb_ref[...]
    is_f = (lane >= SMALL_MF) & (lane < SMALL_MF + 2 * M_HEADS)
    small_o[...] = jnp.where(is_f, _log_sigmoid(zb), zb)
    la_o[...] = _log_sigmoid(_dot(zs.astype(BF16), wg_ref[...]) + bg_ref[...]) * (1.0 / G_TAU)


def _inproj(xs, mod, gain, w_in, wg, bg, qn, kn, cos, sin, sb, conv_w, conv_b, *, n_batch, rows_per_batch, ctx_len):
    tm = INPROJ_TM
    tpb = rows_per_batch // tm
    n_tok, d = xs.shape
    halo = F32_SUBLANES
    tiles_h, n_h = tm // halo, n_tok // halo

    def tok(width):
        return pl.BlockSpec((tm, width), lambda b, t: (b * tpb + t, 0))

    before = pl.BlockSpec((halo, d), lambda b, t: (jnp.maximum((b * tpb + t) * tiles_h - 1, 0), 0))
    after = pl.BlockSpec((halo, d), lambda b, t: (jnp.minimum((b * tpb + t + 1) * tiles_h, n_h - 1), 0))
    pos = pl.BlockSpec((tm, LANES), lambda b, t: (t, 0))
    widths = (A_Q, 4 * LANES, 4 * LANES, G_QK, G_QK, G_V, G_V, M_W, M_W, M_W, M_W, 3 * D_MODEL)
    out_shape = [jax.ShapeDtypeStruct((n_tok, w), BF16) for w in widths]
    out_shape += [jax.ShapeDtypeStruct((n_tok, LANES), F32), jax.ShapeDtypeStruct((n_tok, 2 * G_QK), F32)]
    out_specs = [tok(w) for w in widths] + [tok(LANES), tok(2 * G_QK)]
    consts = (mod, gain, w_in, wg, bg, qn, kn)
    return pl.pallas_call(
        functools.partial(_inproj_kernel, tm=tm, rows_per_batch=rows_per_batch, ctx_len=ctx_len),
        grid=(n_batch, tpb),
        in_specs=([tok(d), before, after] + [_resident(a.shape) for a in consts]
                  + [pos, pos, _resident(sb.shape), _resident(conv_w.shape), _resident(conv_b.shape)]),
        out_specs=out_specs,
        out_shape=out_shape,
        scratch_shapes=[pltpu.VMEM((tm, d), BF16), pltpu.VMEM((tm + 2 * halo, 2 * M_W), F32)],
        compiler_params=pltpu.CompilerParams(dimension_semantics=("parallel", "parallel"),
                                             vmem_limit_bytes=VMEM_LIMIT_BYTES),
        name="inproj",
    )(xs, xs, xs, *consts, cos, sin, sb, conv_w, conv_b)


def _attn_sample(sink_ref, q_ref, k_refs, v_refs, o_ref, bias):
    blk = A_BLOCK
    top = lax.broadcasted_iota(jnp.int32, (2 * blk, 1), 0) < blk
    k_all = jnp.concatenate([ref[...] for ref in k_refs], axis=0)
    v_all = jnp.concatenate([ref[...] for ref in v_refs], axis=0)
    heads_per_kv = A_HEADS // A_KV_HEADS
    combos = [(g, half) for g in range(A_KV_HEADS) for half in range(2)]
    p, sink_term = {}, {}
    for g, half in combos:
        c0 = g * 2 * LANES
        qg = jnp.concatenate([q_ref[:, c0:c0 + LANES], q_ref[:, c0 + LANES:c0 + 2 * LANES]], axis=0)
        col = (2 * g + half) * LANES
        s = _dot_nt(qg, k_all[:, col:col + LANES]) + bias
        sink = jnp.where(top, sink_ref[heads_per_kv * g + half], sink_ref[heads_per_kv * g + 2 + half])
        m = jnp.maximum(jnp.max(s, axis=-1, keepdims=True), sink)
        p[g, half] = jnp.exp(s - m).astype(BF16)
        sink_term[g, half] = jnp.exp(sink - m)
    yield
    lo = lax.broadcasted_iota(jnp.int32, (2 * blk, LANES), 1) < A_HEAD_DIM
    for g in range(A_KV_HEADS):
        c0 = g * 2 * LANES
        scaled = []
        for half in range(2):
            col = (2 * g + half) * LANES
            o = _dot(p[g, half], v_all[:, col:col + LANES])
            ones_lane = _ones_lane(half)
            denom = o[:, ones_lane:ones_lane + 1] + sink_term[g, half]
            scaled.append(o * (1.0 / denom))
        acc = jnp.where(lo, scaled[0], scaled[1])
        o_ref[:, c0:c0 + LANES] = acc[:blk].astype(BF16)
        o_ref[:, c0 + LANES:c0 + 2 * LANES] = acc[blk:].astype(BF16)
    yield


def _attn_kernel(sink_ref, q_ref, kp_ref, kc_ref, kn_ref, kx_ref, vp_ref, vc_ref, vn_ref, vx_ref, o_ref,
                 *, n_ctx_blk, n_blk):
    j = pl.program_id(1)
    blk = A_BLOCK
    off = jnp.int32(4 * blk)
    cur_off = jnp.where(j >= n_ctx_blk, 0, off)
    prev_off = jnp.where(j >= n_ctx_blk + 1, 0, off)
    next_off = jnp.where((j >= n_ctx_blk) & (j <= n_blk - 2), 0, off)
    r = lax.broadcasted_iota(jnp.int32, (blk, blk), 0)
    c = lax.broadcasted_iota(jnp.int32, (blk, blk), 1)
    neg = -jnp.inf
    bias = jnp.concatenate([
        jnp.where(c >= r + prev_off, 0.0, neg),
        jnp.where(c >= cur_off, 0.0, neg),
        jnp.where(c <= r - next_off, 0.0, neg),
        jnp.zeros((blk, kx_ref.shape[1]), F32)], axis=1)
    bias = jnp.concatenate([bias, bias], axis=0)
    _run_staged([
        _attn_sample(sink_ref, q_ref.at[g], [ref.at[g] for ref in (kp_ref, kc_ref, kn_ref, kx_ref)],
                     [ref.at[g] for ref in (vp_ref, vc_ref, vn_ref, vx_ref)], o_ref.at[g], bias)
        for g in range(q_ref.shape[0])])


def _attention(sink, q, k4, v4, *, n_batch, rows_per_batch, ctx_len):
    blk, grp = A_BLOCK, ATTN_BATCH_GROUP
    n_blk = rows_per_batch // blk
    n_tok, width = q.shape
    q, k4, v4 = (_per_sample(a, n_batch) for a in (q, k4, v4))
    cur = lambda b, j: (b, j, 0)
    prev = lambda b, j: (b, jnp.maximum(j - 1, 0), 0)
    nxt = lambda b, j: (b, jnp.minimum(j + 1, n_blk - 1), 0)
    ctx = lambda b, j: (b, 0, 0)
    kv_specs = [pl.BlockSpec((grp, blk, width), prev), pl.BlockSpec((grp, blk, width), cur),
                pl.BlockSpec((grp, blk, width), nxt), pl.BlockSpec((grp, ctx_len, width), ctx)]
    out = pl.pallas_call(
        functools.partial(_attn_kernel, n_ctx_blk=ctx_len // blk, n_blk=n_blk),
        grid=(n_batch // grp, n_blk),
        in_specs=([pl.BlockSpec(memory_space=pltpu.SMEM), pl.BlockSpec((grp, blk, width), cur)]
                  + kv_specs + kv_specs),
        out_specs=pl.BlockSpec((grp, blk, width), cur),
        out_shape=jax.ShapeDtypeStruct((n_batch, rows_per_batch, width), BF16),
        compiler_params=pltpu.CompilerParams(dimension_semantics=("parallel", "parallel"),
                                             vmem_limit_bytes=VMEM_LIMIT_BYTES),
        name="attention",
    )(sink, q, k4, k4, k4, k4, v4, v4, v4, v4)
    return out.reshape(n_tok, width)


def _gla_direction(q_ref, k_ref, v_ref, la_ref, o_ref, st_ref, *, reverse):
    n = q_ref.shape[0]
    r = lax.broadcasted_iota(jnp.int32, (n, n), 0)
    c = lax.broadcasted_iota(jnp.int32, (n, n), 1)
    keep = (c >= r) if reverse else (c <= r)
    last = 0 if reverse else n - 1
    tri = jnp.where(keep, 1.0, 0.0).astype(BF16)
    cum = _cumdot(tri, la_ref[...])
    mid = cum[n // 2:n // 2 + 1, :]
    end = cum[last:last + 1, :]
    qf, kf = q_ref[...].astype(F32), k_ref[...].astype(F32)
    q_in = (qf * jnp.exp(cum - mid)).astype(BF16)
    k_in = (kf * jnp.exp(mid - cum)).astype(BF16)
    q_st = (qf * jnp.exp(cum)).astype(BF16)
    k_st = (kf * jnp.exp(end - cum)).astype(BF16)
    decay = jnp.exp(end)
    lane = lax.broadcasted_iota(jnp.int32, (n, LANES), 1)
    lo = lane < G_DK
    zero = jnp.zeros((), BF16)
    rr = lax.broadcasted_iota(jnp.int32, (2 * G_DV, 2 * G_DK), 0)
    cc = lax.broadcasted_iota(jnp.int32, (2 * G_DV, 2 * G_DK), 1)
    diag = (rr < G_DV) == (cc < G_DK)
    slabs = range(G_HEADS // 2)
    sls = [slice(s * LANES, (s + 1) * LANES) for s in slabs]
    yield
    a = []
    for s in slabs:
        qs, ks = q_in[:, sls[s]], k_in[:, sls[s]]
        a.append((jnp.where(keep, _dot_nt(qs, jnp.where(lo, ks, zero)), 0.0).astype(BF16),
                  jnp.where(keep, _dot_nt(qs, jnp.where(lo, zero, ks)), 0.0).astype(BF16)))
    yield
    for s in slabs:
        inter = _dot_nt(q_st[:, sls[s]], st_ref[s].astype(BF16))
        for half in range(2):
            vsl = slice((2 * s + half) * G_DV, (2 * s + half + 1) * G_DV)
            o_ref[:, vsl] = (_dot(a[s][half], v_ref[:, vsl])
                             + inter[:, half * G_DV:(half + 1) * G_DV]).astype(o_ref.dtype)
    yield
    for s in slabs:
        upd = _dot_tn(v_ref[:, 2 * s * G_DV:(2 * s + 2) * G_DV], k_st[:, sls[s]])
        st_ref[s] = decay[:, sls[s]] * st_ref[s] + jnp.where(diag, upd, 0.0)
    yield


def _run_staged(generators):
    for _ in zip(*generators):
        pass


def _gla_kernel(qf_ref, kf_ref, vf_ref, laf_ref, qb_ref, kb_ref, vb_ref, lab_ref, of_ref, ob_ref, sf_ref, sb_ref):
    @pl.when(pl.program_id(1) == 0)
    def _():
        sf_ref[...] = jnp.zeros_like(sf_ref)
        sb_ref[...] = jnp.zeros_like(sb_ref)

    chains = []
    for g in range(qf_ref.shape[0]):
        chains.append(_gla_direction(qf_ref.at[g], kf_ref.at[g], vf_ref.at[g], laf_ref.at[g], of_ref.at[g],
                                     sf_ref.at[g], reverse=False))
        chains.append(_gla_direction(qb_ref.at[g], kb_ref.at[g], vb_ref.at[g], lab_ref.at[g], ob_ref.at[g],
                                     sb_ref.at[g], reverse=True))
    _run_staged(chains)


def _scan_maps(n_chunks, n_ctx_chunks):
    def fwd(i):
        return i

    def bwd(i):
        return jnp.where(i < n_ctx_chunks, n_ctx_chunks - 1 - i, n_chunks - 1 + n_ctx_chunks - i)

    return fwd, bwd


def _per_sample(a, n_batch):
    return a.reshape(n_batch, a.shape[0] // n_batch, a.shape[1])


def _gla(gq, gk, gv, la, *, n_batch, rows_per_batch, ctx_len):
    n, grp = GLA_CHUNK, SCAN_BATCH_GROUP
    n_chunks = rows_per_batch // n
    fwd, bwd = _scan_maps(n_chunks, ctx_len // n)
    n_tok = gq.shape[0]
    gq, gk, gv, la = (_per_sample(a, n_batch) for a in (gq, gk, gv, la))

    def specs(chunk, la_col):
        return [pl.BlockSpec((grp, n, G_QK), lambda b, i: (b, chunk(i), 0)),
                pl.BlockSpec((grp, n, G_QK), lambda b, i: (b, chunk(i), 0)),
                pl.BlockSpec((grp, n, G_V), lambda b, i: (b, chunk(i), 0)),
                pl.BlockSpec((grp, n, G_QK), lambda b, i: (b, chunk(i), la_col))]

    state = pltpu.VMEM((grp, G_HEADS // 2, 2 * G_DV, 2 * G_DK), F32)
    out_f, out_b = pl.pallas_call(
        _gla_kernel,
        grid=(n_batch // grp, n_chunks),
        in_specs=specs(fwd, 0) + specs(bwd, 1),
        out_specs=[pl.BlockSpec((grp, n, G_V), lambda b, i: (b, fwd(i), 0)),
                   pl.BlockSpec((grp, n, G_V), lambda b, i: (b, bwd(i), 0))],
        out_shape=[jax.ShapeDtypeStruct((n_batch, rows_per_batch, G_V), BF16)] * 2,
        scratch_shapes=[state, state],
        compiler_params=pltpu.CompilerParams(dimension_semantics=("arbitrary", "arbitrary"),
                                             vmem_limit_bytes=VMEM_LIMIT_BYTES),
        name="gla",
    )(gq, gk, gv, la, gq, gk, gv, la)
    return out_f.reshape(n_tok, G_V), out_b.reshape(n_tok, G_V)


def _cumdot_right(x, tri):
    hi = x.astype(BF16)
    r1 = x - hi.astype(F32)
    mid = r1.astype(BF16)
    lo = (r1 - mid.astype(F32)).astype(BF16)
    return _dot(hi, tri) + _dot(mid, tri) + _dot(lo, tri)


def _mlstm_direction(q_ref, k_ref, v_ref, small_ref, o_ref, ct_ref, n_ref, m_ref, *, direction):
    n = q_ref.shape[0]
    reverse = direction == 1
    n_gates = 2 * M_HEADS
    jj = lax.broadcasted_iota(jnp.int32, (n, n), 0)
    ii = lax.broadcasted_iota(jnp.int32, (n, n), 1)
    vis = (jj >= ii) if reverse else (jj <= ii)
    last = 0 if reverse else n - 1
    vis_bf = jnp.where(vis, 1.0, 0.0).astype(BF16)
    gates_t = small_ref[...].T[SMALL_MI:SMALL_MI + 2 * n_gates, :]
    b_rows = _cumdot_right(gates_t, vis_bf)[n_gates:, :]
    u_rows = gates_t[:n_gates, :] - b_rows
    m_prev = m_ref[direction][:, 0:1]
    lane = lax.broadcasted_iota(jnp.int32, (n_gates, n), 1)
    m_rel = u_rows
    for step in range(n.bit_length() - 1):
        s = 1 << step
        if reverse:
            shifted = jnp.where(lane < n - s, pltpu.roll(m_rel, n - s, 1), -jnp.inf)
        else:
            shifted = jnp.where(lane >= s, pltpu.roll(m_rel, s, 1), -jnp.inf)
        m_rel = jnp.maximum(m_rel, shifted)
    m_rel = jnp.maximum(m_rel, m_prev)
    w_inter = jnp.exp(m_prev - m_rel)
    m_rows = b_rows + m_rel
    inv_floor = jnp.exp(-m_rows)
    m_new = m_rows[:, last:last + 1]
    b_end = b_rows[:, last:last + 1]
    decay = jnp.exp(b_end + m_prev - m_new)
    m_ref[direction] = jnp.broadcast_to(m_new, m_ref.shape[1:])
    log2e = 1.4426950408889634
    m2_rows = m_rel * log2e
    c2 = (b_end - m_new) * log2e
    u2_cols = jnp.concatenate([u_rows * log2e, jnp.zeros((LANES - n_gates, n), F32)], axis=0).T
    v_t = v_ref[...].T
    ones = jnp.ones((8, n), BF16)
    heads = range(M_HEADS)
    sls = [slice(h * M_HEAD_DIM, (h + 1) * M_HEAD_DIM) for h in heads]
    rows = [slice(direction * M_HEADS + h, direction * M_HEADS + h + 1) for h in heads]
    u2, s_t = [], []
    for h in heads:
        u2.append(jnp.broadcast_to(u2_cols[:, rows[h]], (n, n)))
        w_t = jnp.exp2(jnp.where(vis, u2[h] - m2_rows[rows[h], :], -jnp.inf))
        s_t.append((_dot_nt(k_ref[:, sls[h]], q_ref[:, sls[h]]) * w_t).astype(BF16))
    yield
    for h in heads:
        idx = direction * M_HEADS + h
        qh = q_ref[:, sls[h]]
        nq = _dot_nt(n_ref[idx].astype(BF16), qh)[0:1, :]
        den = w_inter[rows[h], :] * nq + _dot(ones, s_t[h])[0:1, :]
        norm = 1.0 / jnp.maximum(jnp.abs(den), inv_floor[rows[h], :])
        h_t = (_dot(v_t[sls[h], :], s_t[h])
               + _dot_nt(ct_ref[idx].astype(BF16), qh) * w_inter[rows[h], :]) * norm
        o_ref[:, sls[h]] = h_t.astype(o_ref.dtype).T
    yield
    for h in heads:
        idx = direction * M_HEADS + h
        kw = k_ref[:, sls[h]].astype(F32) * jnp.exp2(u2[h][:, :M_HEAD_DIM] + c2[rows[h], :])
        ct_ref[idx] = decay[rows[h], :] * ct_ref[idx] + _dot(v_t[sls[h], :], kw.astype(BF16))
        n_new = decay[rows[h], :] * n_ref[idx][0:1, :] + jnp.sum(kw, axis=0, keepdims=True)
        n_ref[idx] = jnp.broadcast_to(n_new, n_ref.shape[1:])
    yield


def _mlstm_kernel(qf_ref, kf_ref, vf_ref, sf_ref, qb_ref, kb_ref, vb_ref, sb_ref, of_ref, ob_ref,
                  c_ref, n_ref, m_ref):
    @pl.when(pl.program_id(1) == 0)
    def _():
        c_ref[...] = jnp.zeros_like(c_ref)
        n_ref[...] = jnp.zeros_like(n_ref)
        m_ref[...] = jnp.zeros_like(m_ref)

    chains = []
    for g in range(qf_ref.shape[0]):
        state = (c_ref.at[g], n_ref.at[g], m_ref.at[g])
        chains.append(_mlstm_direction(qf_ref.at[g], kf_ref.at[g], vf_ref.at[g], sf_ref.at[g], of_ref.at[g],
                                       *state, direction=0))
        chains.append(_mlstm_direction(qb_ref.at[g], kb_ref.at[g], vb_ref.at[g], sb_ref.at[g], ob_ref.at[g],
                                       *state, direction=1))
    _run_staged(chains)


def _mlstm(mq, mk, mv, small, *, n_batch, rows_per_batch, ctx_len):
    n, grp = MLSTM_CHUNK, SCAN_BATCH_GROUP
    n_chunks = rows_per_batch // n
    fwd, bwd = _scan_maps(n_chunks, ctx_len // n)
    n_tok = mq.shape[0]
    mq, mk, mv, small = (_per_sample(a, n_batch) for a in (mq, mk, mv, small))

    def specs(chunk):
        return ([pl.BlockSpec((grp, n, M_W), lambda b, i: (b, chunk(i), 0))] * 3
                + [pl.BlockSpec((grp, n, LANES), lambda b, i: (b, chunk(i), 0))])

    out_f, out_b = pl.pallas_call(
        _mlstm_kernel,
        grid=(n_batch // grp, n_chunks),
        in_specs=specs(fwd) + specs(bwd),
        out_specs=[pl.BlockSpec((grp, n, M_W), lambda b, i: (b, fwd(i), 0)),
                   pl.BlockSpec((grp, n, M_W), lambda b, i: (b, bwd(i), 0))],
        out_shape=[jax.ShapeDtypeStruct((n_batch, rows_per_batch, M_W), BF16)] * 2,
        scratch_shapes=[pltpu.VMEM((grp, 2 * M_HEADS, M_HEAD_DIM, M_HEAD_DIM), F32),
                        pltpu.VMEM((grp, 2 * M_HEADS, 8, M_HEAD_DIM), F32),
                        pltpu.VMEM((grp, 2, 2 * M_HEADS, LANES), F32)],
        compiler_params=pltpu.CompilerParams(dimension_semantics=("arbitrary", "arbitrary"),
                                             vmem_limit_bytes=VMEM_LIMIT_BYTES),
        name="mlstm",
    )(mq, mk, mv, small, mq, mk, mv, small)
    return out_f.reshape(n_tok, M_W), out_b.reshape(n_tok, M_W)


def _headnorm128(x, gain):
    parts = [_rmsnorm(x[:, s * LANES:(s + 1) * LANES], gain) for s in range(x.shape[1] // LANES)]
    return jnp.concatenate(parts, axis=1)


def _merge_ffn_kernel(x_ref, att_ref, gf_ref, gb_ref, hf_ref, hb_ref, gr_ref, mo_ref, gates_ref,
                      mod_ref, gn_ref, mn_ref, wa_ref, wg_ref, wm_ref, wo_ref, g_ref, w13_ref, w2_ref,
                      o_ref, h_ref, acc_ref, *, tm, ctx_len, row_base):
    b, t = pl.program_id(0), pl.program_id(1)
    d = x_ref.shape[1]
    row = _tile_rows(t, tm, row_base)
    gla = gf_ref[...].astype(F32) + gb_ref[...].astype(F32)
    g = (_headnorm128(gla, gn_ref[...]) * gr_ref[...].astype(F32)).astype(BF16)
    mls = hf_ref[...].astype(F32) + hb_ref[...].astype(F32)
    m = (_headnorm128(mls, mn_ref[...]) * mo_ref[...].astype(F32)).astype(BF16)
    y = (gates_ref[:, 0:d].astype(F32) * _dot(att_ref[...], wa_ref[...])
         + gates_ref[:, d:2 * d].astype(F32) * _dot(g, wg_ref[...])
         + gates_ref[:, 2 * d:3 * d].astype(F32) * _dot(m, wm_ref[...]))
    _, _, gate = _mod_rows(mod_ref, 1, b, row, ctx_len)
    x = x_ref[...] + gate * _dot(y.astype(BF16), wo_ref[...])
    first_row = row_base if row_base >= ctx_len else row_base + t * tm
    _adaln_tile(h_ref, x, g_ref[...], mod_ref, 2, b, row, ctx_len, first_row)
    _, _, gate = _mod_rows(mod_ref, 2, b, row, ctx_len)
    _swiglu(h_ref, w13_ref, w2_ref, acc_ref)
    o_ref[...] = x + (0.5 * gate) * acc_ref[...]


def _merge_ffn(streams, consts, *, n_batch, rows_per_batch, ctx_len, latent_only):
    xs = streams[0]
    d = xs.shape[1]
    if latent_only:
        tm = MERGE_TM_LATENT
        n_rows = rows_per_batch - ctx_len
        tpb = n_rows // tm

        def tok(width):
            align = math.gcd(rows_per_batch, ctx_len, tm)
            return pl.BlockSpec((pl.Element(tm), pl.Element(width)),
                                lambda b, t: (pl.multiple_of(b * rows_per_batch + ctx_len + t * tm, align), 0))
    else:
        tm = INPROJ_TM
        n_rows = rows_per_batch
        tpb = n_rows // tm

        def tok(width):
            return pl.BlockSpec((tm, width), lambda b, t: (b * tpb + t, 0))

    return pl.pallas_call(
        functools.partial(_merge_ffn_kernel, tm=tm, ctx_len=ctx_len, row_base=ctx_len if latent_only else 0),
        grid=(n_batch, tpb),
        in_specs=[tok(a.shape[1]) for a in streams] + [_resident(a.shape) for a in consts],
        out_specs=pl.BlockSpec((tm, d), lambda b, t: (b * tpb + t, 0)),
        out_shape=jax.ShapeDtypeStruct((n_batch * n_rows, d), F32),
        scratch_shapes=[pltpu.VMEM((tm, d), BF16), pltpu.VMEM((tm, d), F32)],
        compiler_params=pltpu.CompilerParams(dimension_semantics=("parallel", "parallel"),
                                             vmem_limit_bytes=VMEM_LIMIT_BYTES),
        name="merge_ffn",
    )(*streams, *consts)


def _rope_tables(seq, ctx_len):
    rows = seq // GRID_W
    r = jnp.repeat(jnp.arange(rows), GRID_W).astype(F32)
    col = jnp.tile(jnp.arange(GRID_W), rows).astype(F32)
    n_freq = A_HEAD_DIM // 4
    inv = ROPE_BASE ** (-jnp.arange(n_freq, dtype=F32) / n_freq)
    ang = jnp.concatenate([r[:, None] * inv, col[:, None] * inv], axis=-1)
    cos, sin = jnp.cos(ang), jnp.sin(ang)
    cos = jnp.concatenate([cos, cos, cos, cos], axis=-1)
    sin = jnp.concatenate([-sin, sin, -sin, sin], axis=-1)
    cos = jnp.concatenate([jnp.ones((ctx_len, LANES), F32), cos], axis=0)
    sin = jnp.concatenate([jnp.zeros((ctx_len, LANES), F32), sin], axis=0)
    return cos, sin


def _reorder_w_in(w):
    idx = np.cumsum((0,) + IN_SPLITS)
    (aq, ak, av, gq, gk, gv, gr, gg, mq, mk, mv, mo, mi, mf, s_a, s_g, s_m) = [
        w[:, int(idx[i]):int(idx[i + 1])] for i in range(len(IN_SPLITS))]
    pad = jnp.zeros((w.shape[0], LANES - gg.shape[1] - mi.shape[1] - mf.shape[1]), w.dtype)
    return jnp.concatenate([aq, ak, av, gq, gk, gv, gr, mq, mk, mv, mo, s_a, s_g, s_m, gg, mi, mf, pad], axis=1)


def kernel(x, c, ctx, c_ctx, mod_w, mod_b, norm_g, ffn1_w13, ffn1_w2, ffn2_w13, ffn2_w2, w_in, attn_q_norm,
           attn_k_norm, attn_sink, gla_w2, gla_b, gla_norm, mlstm_conv_w, mlstm_conv_b, mlstm_ib, mlstm_fb,
           mlstm_norm, w_out_attn, w_out_gla, w_out_mlstm, w_o):
    n_batch, seq, d = x.shape
    ctx_len = ctx.shape[1]
    depth = mod_w.shape[0]
    rows_per_batch = ctx_len + seq
    assert d == D_MODEL and n_batch < MOD_ROWS
    assert rows_per_batch % FFN_TM == 0 and rows_per_batch % INPROJ_TM == 0
    assert ctx_len % MLSTM_CHUNK == 0 and seq % MLSTM_CHUNK == 0 and seq % GRID_W == 0
    assert seq % MERGE_TM_LATENT == 0 and n_batch % SCAN_BATCH_GROUP == 0 and n_batch % ATTN_BATCH_GROUP == 0
    geo = dict(n_batch=n_batch, rows_per_batch=rows_per_batch, ctx_len=ctx_len)

    xs = jnp.concatenate([ctx, x], axis=1).reshape(n_batch * rows_per_batch, d)
    cc = jnp.concatenate([c, jnp.broadcast_to(c_ctx[None], (MOD_ROWS - n_batch, d))], axis=0)
    mod_all = _modulation(cc, mod_w, mod_b)[:, :, :n_batch + 1]
    cos, sin = _rope_tables(seq, ctx_len)

    for l in range(depth):
        mod = mod_all[l]
        gains = norm_g[l]
        xs = _ffn(xs, mod, gains[0:1], ffn1_w13[l].astype(BF16), ffn1_w2[l].astype(BF16), j=0, **geo)

        wg = jnp.zeros((LANES, 2 * G_QK), F32)
        wg = wg.at[0:G_RANK, 0:G_QK].set(gla_w2[l, 0]).at[G_RANK:2 * G_RANK, G_QK:].set(gla_w2[l, 1])
        sb = jnp.zeros((1, LANES), F32)
        sb = sb.at[0, SMALL_MI:SMALL_MI + 2 * M_HEADS].set(mlstm_ib[l].reshape(-1))
        sb = sb.at[0, SMALL_MF:SMALL_MF + 2 * M_HEADS].set(mlstm_fb[l].reshape(-1))
        (q, k4, v4, gq, gk, gv, gr, mq, mk, mv, mo, gates, small, la) = _inproj(
            xs, mod, gains[1:2], _reorder_w_in(w_in[l]).astype(BF16), wg.astype(BF16),
            gla_b[l].reshape(1, 2 * G_QK), jnp.tile(attn_q_norm[l], 2)[None], jnp.tile(attn_k_norm[l], 2)[None],
            cos, sin, sb, mlstm_conv_w[l], mlstm_conv_b[l][None], **geo)

        att = _attention(attn_sink[l], q, k4, v4, **geo)
        gf, gb = _gla(gq, gk, gv, la, **geo)
        hf, hb = _mlstm(mq, mk, mv, small, **geo)
        consts = (mod, gla_norm[l][None], mlstm_norm[l][None], w_out_attn[l].astype(BF16),
                  w_out_gla[l].astype(BF16), w_out_mlstm[l].astype(BF16), w_o[l].astype(BF16),
                  gains[2:3], ffn2_w13[l].astype(BF16), ffn2_w2[l].astype(BF16))
        xs = _merge_ffn((xs, att, gf, gb, hf, hb, gr, mo, gates), consts, latent_only=l == depth - 1, **geo)

    return xs.reshape(n_batch, seq, d)
```

```python
import functools
import math

import jax
import jax.numpy as jnp
import numpy as np
from jax import lax
from jax.experimental import pallas as pl
from jax.experimental.pallas import tpu as pltpu

F32 = jnp.float32
BF16 = jnp.bfloat16

D_MODEL = 1024
GRID_W = 64
A_HEADS, A_KV_HEADS, A_HEAD_DIM = 8, 2, 64
WINDOW = A_BLOCK = 128
ROPE_BASE = 10000.0
G_HEADS, G_DK, G_DV, G_RANK, G_TAU = 4, 64, 128, 16, 16.0
M_HEADS, M_HEAD_DIM, M_CONV = 4, 128, 5
D_FF = 2816
N_MOD = 9
EPS = 1e-6
A_Q = A_HEADS * A_HEAD_DIM
A_KV = A_KV_HEADS * A_HEAD_DIM
G_QK = G_HEADS * G_DK
G_V = G_HEADS * G_DV
M_W = M_HEADS * M_HEAD_DIM
IN_SPLITS = (A_Q, A_KV, A_KV, G_QK, G_QK, G_V, G_V, 2 * G_RANK,
             M_W, M_W, M_W, M_W, 2 * M_HEADS, 2 * M_HEADS, D_MODEL, D_MODEL, D_MODEL)

LANES = 128
F32_SUBLANES = 8
BF16_SUBLANES = 16
VMEM_LIMIT_BYTES = 56 * 1024 * 1024

FFN_TM = 768
INPROJ_TM = 384
MERGE_TM_LATENT = 512
FFN_FK = 256
GLA_CHUNK = 128
MLSTM_CHUNK = 256
SCAN_BATCH_GROUP = 4
ATTN_BATCH_GROUP = 4
MOD_ROWS = 8

SMALL_GG = 0
SMALL_MI = 2 * G_RANK
SMALL_MF = SMALL_MI + 2 * M_HEADS

_OFF = {}
_o = 0
for _name, _w in (("aq", A_Q), ("akv", 2 * A_KV), ("gqk", 2 * G_QK), ("gv", G_V), ("gr", G_V),
                  ("mqk", 2 * M_W), ("mv", M_W), ("mo", M_W), ("gates", 3 * D_MODEL), ("small", LANES)):
    _OFF[_name] = (_o, _o + _w)
    _o += _w
D_IN_PAD = _o


def _dot(a, b):
    return jnp.dot(a, b, preferred_element_type=F32)


def _dot_nt(a, b):
    return lax.dot_general(a, b, (((1,), (1,)), ((), ())), preferred_element_type=F32)


def _dot_tn(a, b):
    return lax.dot_general(a, b, (((0,), (0,)), ((), ())), preferred_element_type=F32)


def _cumdot(tri, x):
    hi = x.astype(BF16)
    r1 = x - hi.astype(F32)
    mid = r1.astype(BF16)
    lo = (r1 - mid.astype(F32)).astype(BF16)
    return _dot(tri, hi) + _dot(tri, mid) + _dot(tri, lo)


def _sigmoid(x):
    return 0.5 * jnp.tanh(0.5 * x) + 0.5


def _silu(x):
    h = 0.5 * x
    return h * jnp.tanh(h) + h


def _log_sigmoid(x):
    return jnp.minimum(x, 0.0) - jnp.log(1.0 + jnp.exp(-jnp.abs(x)))


def _rmsnorm(x, gain):
    return x * lax.rsqrt(jnp.mean(x * x, axis=-1, keepdims=True) + EPS) * gain


def _tile_rows(t, tm, base=0):
    return base + t * tm + lax.broadcasted_iota(jnp.int32, (tm, 1), 0)


def _adaln(x, gain, shift, scale):
    return (_rmsnorm(x, gain) * (1.0 + scale) + shift).astype(BF16)


def _adaln_tile(h_ref, x, gain, mod_ref, j, b, row, ctx_len, first_row):
    def latent_only():
        shift = mod_ref[3 * j, pl.ds(b, 1), :]
        scale = mod_ref[3 * j + 1, pl.ds(b, 1), :]
        inv = lax.rsqrt(jnp.mean(x * x, axis=-1, keepdims=True) + EPS)
        h_ref[...] = (x * inv * (gain * (1.0 + scale)) + shift).astype(BF16)

    def mixed():
        shift, scale, _ = _mod_rows(mod_ref, j, b, row, ctx_len)
        h_ref[...] = _adaln(x, gain, shift, scale)

    if isinstance(first_row, int):
        latent_only() if first_row >= ctx_len else mixed()
    else:
        pl.when(first_row >= ctx_len)(latent_only)
        pl.when(first_row < ctx_len)(mixed)


def _mod_rows(mod_ref, j, b, row, ctx_len):
    is_ctx = row < ctx_len
    n_b = mod_ref.shape[1] - 1
    out = []
    for q in range(3):
        lat = mod_ref[3 * j + q, pl.ds(b, 1), :]
        ctx = mod_ref[3 * j + q, n_b:n_b + 1, :]
        out.append(jnp.where(is_ctx, ctx, lat))
    return out


def _mod_kernel(c_ref, w_ref, b_ref, o_ref):
    h = _silu(c_ref[...]).astype(BF16)
    o_ref[...] = _dot(h, w_ref[...].astype(BF16)) + b_ref[...]


def _modulation(cc, mod_w, mod_b):
    depth, d, _ = mod_w.shape
    rows = cc.shape[0]
    return pl.pallas_call(
        _mod_kernel,
        grid=(depth, N_MOD),
        in_specs=[
            pl.BlockSpec((rows, d), lambda l, j: (0, 0)),
            pl.BlockSpec((None, d, d), lambda l, j: (l, 0, j)),
            pl.BlockSpec((None, None, 1, d), lambda l, j: (l, j, 0, 0)),
        ],
        out_specs=pl.BlockSpec((None, None, rows, d), lambda l, j: (l, j, 0, 0)),
        out_shape=jax.ShapeDtypeStruct((depth, N_MOD, rows, d), F32),
        compiler_params=pltpu.CompilerParams(dimension_semantics=("arbitrary", "arbitrary")),
        name="modulation",
    )(cc, mod_w, mod_b.reshape(depth, N_MOD, 1, d))


def _swiglu(h_ref, w13_ref, w2_ref, acc_ref):
    for k in range(D_FF // FFN_FK):
        lo, hi = k * FFN_FK, (k + 1) * FFN_FK
        a = _dot(h_ref[...], w13_ref[:, lo:hi])
        g = _dot(h_ref[...], w13_ref[:, D_FF + lo:D_FF + hi])
        contrib = _dot((_silu(a) * g).astype(BF16), w2_ref[lo:hi, :])
        if k == 0:
            acc_ref[...] = contrib
        else:
            acc_ref[...] += contrib


def _ffn_kernel(x_ref, mod_ref, g_ref, w13_ref, w2_ref, o_ref, h_ref, acc_ref, *, j, tm, ctx_len):
    b, t = pl.program_id(0), pl.program_id(1)
    x = x_ref[...]
    row = _tile_rows(t, tm)
    _adaln_tile(h_ref, x, g_ref[...], mod_ref, j, b, row, ctx_len, t * tm)
    _, _, gate = _mod_rows(mod_ref, j, b, row, ctx_len)
    _swiglu(h_ref, w13_ref, w2_ref, acc_ref)
    o_ref[...] = x + (0.5 * gate) * acc_ref[...]


def _resident(shape):
    return pl.BlockSpec(shape, lambda *_: (0,) * len(shape), pipeline_mode=pl.Buffered(1))


def _ffn(xs, mod, gain, w13, w2, *, j, n_batch, rows_per_batch, ctx_len):
    tm = FFN_TM
    tpb = rows_per_batch // tm
    d = xs.shape[1]
    tok = pl.BlockSpec((tm, d), lambda b, t: (b * tpb + t, 0))
    return pl.pallas_call(
        functools.partial(_ffn_kernel, j=j, tm=tm, ctx_len=ctx_len),
        grid=(n_batch, tpb),
        in_specs=[tok, _resident(mod.shape), _resident(gain.shape), _resident(w13.shape), _resident(w2.shape)],
        out_specs=tok,
        out_shape=jax.ShapeDtypeStruct(xs.shape, F32),
        scratch_shapes=[pltpu.VMEM((tm, d), BF16), pltpu.VMEM((tm, d), F32)],
        compiler_params=pltpu.CompilerParams(dimension_semantics=("parallel", "parallel"),
                                             vmem_limit_bytes=VMEM_LIMIT_BYTES),
        name=f"ffn{j}",
    )(xs, mod, gain, w13, w2)


def _headnorm_rope(z, gain, cos, sin):
    lane = lax.broadcasted_iota(jnp.int32, z.shape, 1)
    lo = lane < A_HEAD_DIM
    sq = z * z
    s_lo = jnp.sum(jnp.where(lo, sq, 0.0), axis=-1, keepdims=True)
    s_hi = jnp.sum(jnp.where(lo, 0.0, sq), axis=-1, keepdims=True)
    inv = lax.rsqrt(jnp.where(lo, s_lo, s_hi) * (1.0 / A_HEAD_DIM) + EPS)
    y = z * inv * gain
    first_half = (lane & (A_HEAD_DIM // 2)) == 0
    partner = jnp.where(first_half, pltpu.roll(y, LANES - A_HEAD_DIM // 2, 1), pltpu.roll(y, A_HEAD_DIM // 2, 1))
    return y * cos + partner * sin


def _spread_kv(z):
    lane = lax.broadcasted_iota(jnp.int32, z.shape, 1)
    lo = lane < A_HEAD_DIM
    a0 = jnp.where(lo, z, 0.0)
    b1 = jnp.where(lo, 0.0, z)
    return [a0, pltpu.roll(a0, A_HEAD_DIM, 1), pltpu.roll(b1, A_HEAD_DIM, 1), b1]


def _ones_lane(half):
    return A_HEAD_DIM if half == 0 else 0


def _with_ones_lane(parts):
    lane = lax.broadcasted_iota(jnp.int32, parts[0].shape, 1)
    return [jnp.where(lane == _ones_lane(i % 2), 1.0, part) for i, part in enumerate(parts)]


def _inproj_kernel(x_ref, xp_ref, xn_ref, mod_ref, g_ref, w_ref, wg_ref, bg_ref, qn_ref, kn_ref, cos_ref, sin_ref,
                   sb_ref, cw_ref, cb_ref,
                   q_o, k4_o, v4_o, gq_o, gk_o, gv_o, gr_o, mq_o, mk_o, mv_o, mo_o, gates_o, small_o, la_o,
                   h_ref, e_ref, *, tm, rows_per_batch, ctx_len):
    b, t = pl.program_id(0), pl.program_id(1)
    row = _tile_rows(t, tm)
    _adaln_tile(h_ref, x_ref[...], g_ref[...], mod_ref, 1, b, row, ctx_len, t * tm)

    def proj(name, lo, hi):
        base = _OFF[name][0]
        return _dot(h_ref[...], w_ref[:, base + lo:base + hi])

    halo = xp_ref.shape[0]
    row_h = jnp.concatenate([_tile_rows(t, tm, -halo)[:halo], _tile_rows(t + 1, tm)[:halo]], axis=0)
    shift_h, scale_h, _ = _mod_rows(mod_ref, 1, b, row_h, ctx_len)
    h_halo = _adaln(jnp.concatenate([xp_ref[...], xn_ref[...]], axis=0), g_ref[...], shift_h, scale_h)
    base = _OFF["mqk"][0]
    z_halo = _dot(h_halo, w_ref[:, base:base + 2 * M_W])
    first = t * tm
    prev_ok = (first != 0) & (first != ctx_len)
    next_ok = (first + tm != ctx_len) & (first + tm != rows_per_batch)
    e_ref[0:halo, :] = z_halo[:halo] * jnp.where(prev_ok, 1.0, 0.0)
    e_ref[halo + tm:, :] = z_halo[halo:] * jnp.where(next_ok, 1.0, 0.0)
    pad = M_CONV // 2
    conv_cols = 2 * LANES

    def conv_rows(r0, n_rows, sl, edge=None):
        acc = None
        for tap in range(M_CONV):
            off = tap - pad
            term = e_ref[pl.ds(halo + r0 + off, n_rows), sl]
            if edge is not None:
                r = r0 + lax.broadcasted_iota(jnp.int32, (n_rows, 1), 0)
                term = jnp.where((r < edge) == (r + off < edge), term, 0.0)
            term = term * cw_ref[tap:tap + 1, sl]
            acc = term if acc is None else acc + term
        return _silu(acc + cb_ref[:, sl])

    def store_qk(r0, n_rows, sl, y):
        if sl.start < M_W:
            mq_o[r0:r0 + n_rows, sl] = y.astype(BF16)
        else:
            mk_o[r0:r0 + n_rows, sl.start - M_W:sl.stop - M_W] = (y * (M_HEAD_DIM ** -0.5)).astype(BF16)

    def group_mqk(c):
        sl = slice(c * conv_cols, (c + 1) * conv_cols)
        e_ref[halo:halo + tm, sl] = proj("mqk", sl.start, sl.stop)
        store_qk(0, tm, sl, conv_rows(0, tm, sl))

    cos, sin = cos_ref[...], sin_ref[...]

    def group_aq():
        z = proj("aq", 0, A_Q)
        for s in range(A_Q // LANES):
            y = _headnorm_rope(z[:, s * LANES:(s + 1) * LANES], qn_ref[...], cos, sin)
            q_o[:, s * LANES:(s + 1) * LANES] = (y * (A_HEAD_DIM ** -0.5)).astype(BF16)

    def group_akv():
        z = proj("akv", 0, 2 * A_KV)
        for s, part in enumerate(_spread_kv(_headnorm_rope(z[:, :A_KV], kn_ref[...], cos, sin))):
            k4_o[:, s * LANES:(s + 1) * LANES] = part.astype(BF16)
        for s, part in enumerate(_with_ones_lane(_spread_kv(z[:, A_KV:]))):
            v4_o[:, s * LANES:(s + 1) * LANES] = part.astype(BF16)

    def group_gqk():
        z = proj("gqk", 0, 2 * G_QK)
        gq_o[...] = (z[:, :G_QK] * (G_DK ** -0.5)).astype(BF16)
        gk_o[...] = z[:, G_QK:].astype(BF16)

    def group_gates(s):
        gates_o[:, s * 512:(s + 1) * 512] = _sigmoid(proj("gates", s * 512, (s + 1) * 512)).astype(BF16)

    def group_small():
        zs = proj("small", 0, LANES)
        lane = lax.broadcasted_iota(jnp.int32, zs.shape, 1)
        zb = zs + sb_ref[...]
        is_f = (lane >= SMALL_MF) & (lane < SMALL_MF + 2 * M_HEADS)
        small_o[...] = jnp.where(is_f, _log_sigmoid(zb), zb)
        la_o[...] = _log_sigmoid(_dot(zs.astype(BF16), wg_ref[...]) + bg_ref[...]) * (1.0 / G_TAU)

    heavy = [functools.partial(group_mqk, c) for c in range(2 * M_W // conv_cols)]
    heavy += [group_small, group_aq, group_akv]
    light = [functools.partial(group_gates, s) for s in range(3 * D_MODEL // 512)] + [group_gqk]
    for i in range(max(len(heavy), len(light))):
        if i < len(heavy):
            heavy[i]()
        if i < len(light):
            light[i]()
    gv_o[...] = proj("gv", 0, G_V).astype(BF16)
    gr_o[...] = _silu(proj("gr", 0, G_V)).astype(BF16)
    mv_o[...] = proj("mv", 0, M_W).astype(BF16)
    mo_o[...] = _sigmoid(proj("mo", 0, M_W)).astype(BF16)

    edge_tile, edge = divmod(ctx_len, tm)
    if edge:
        span = BF16_SUBLANES
        assert edge % span == 0 and span >= pad and span <= edge <= tm - span

        @pl.when(t == edge_tile)
        def _():
            for c in range(2 * M_W // conv_cols):
                sl = slice(c * conv_cols, (c + 1) * conv_cols)
                store_qk(edge - span, 2 * span, sl, conv_rows(edge - span, 2 * span, sl, edge=edge))


def _inproj(xs, mod, gain, w_in, wg, bg, qn, kn, cos, sin, sb, conv_w, conv_b, *, n_batch, rows_per_batch, ctx_len):
    tm = INPROJ_TM
    tpb = rows_per_batch // tm
    n_tok, d = xs.shape
    halo = F32_SUBLANES
    tiles_h, n_h = tm // halo, n_tok // halo

    def tok(width):
        return pl.BlockSpec((tm, width), lambda b, t: (b * tpb + t, 0))

    before = pl.BlockSpec((halo, d), lambda b, t: (jnp.maximum((b * tpb + t) * tiles_h - 1, 0), 0))
    after = pl.BlockSpec((halo, d), lambda b, t: (jnp.minimum((b * tpb + t + 1) * tiles_h, n_h - 1), 0))
    pos = pl.BlockSpec((tm, LANES), lambda b, t: (t, 0))
    widths = (A_Q, 4 * LANES, 4 * LANES, G_QK, G_QK, G_V, G_V, M_W, M_W, M_W, M_W, 3 * D_MODEL)
    out_shape = [jax.ShapeDtypeStruct((n_tok, w), BF16) for w in widths]
    out_shape += [jax.ShapeDtypeStruct((n_tok, LANES), F32), jax.ShapeDtypeStruct((n_tok, 2 * G_QK), F32)]
    out_specs = [tok(w) for w in widths] + [tok(LANES), tok(2 * G_QK)]
    consts = (mod, gain, w_in, wg, bg, qn, kn)
    return pl.pallas_call(
        functools.partial(_inproj_kernel, tm=tm, rows_per_batch=rows_per_batch, ctx_len=ctx_len),
        grid=(n_batch, tpb),
        in_specs=([tok(d), before, after] + [_resident(a.shape) for a in consts]
                  + [pos, pos, _resident(sb.shape), _resident(conv_w.shape), _resident(conv_b.shape)]),
        out_specs=out_specs,
        out_shape=out_shape,
        scratch_shapes=[pltpu.VMEM((tm, d), BF16), pltpu.VMEM((tm + 2 * halo, 2 * M_W), F32)],
        compiler_params=pltpu.CompilerParams(dimension_semantics=("parallel", "parallel"),
                                             vmem_limit_bytes=VMEM_LIMIT_BYTES),
        name="inproj",
    )(xs, xs, xs, *consts, cos, sin, sb, conv_w, conv_b)


def _attn_sample(sink_ref, q_ref, k_refs, v_refs, o_ref, bias):
    blk = A_BLOCK
    k_all = jnp.concatenate([ref[...] for ref in k_refs], axis=0)
    v_all = jnp.concatenate([ref[...] for ref in v_refs], axis=0)
    heads_per_kv = A_HEADS // A_KV_HEADS
    combos = [(g, half) for g in range(A_KV_HEADS) for half in range(2)]
    top = lax.broadcasted_iota(jnp.int32, (2 * blk, 1), 0) < blk
    p, sink_term = {}, {}
    for g, half in combos:
        c0 = g * 2 * LANES
        qg = jnp.concatenate([q_ref[:, c0:c0 + LANES], q_ref[:, c0 + LANES:c0 + 2 * LANES]], axis=0)
        col = (2 * g + half) * LANES
        s = _dot_nt(qg, k_all[:, col:col + LANES]) + bias
        sink = jnp.where(top, sink_ref[heads_per_kv * g + half], sink_ref[heads_per_kv * g + 2 + half])
        m = jnp.maximum(jnp.max(s, axis=-1, keepdims=True), sink)
        p[g, half] = jnp.exp(s - m).astype(BF16)
        sink_term[g, half] = jnp.exp(sink - m)
    yield
    lo = lax.broadcasted_iota(jnp.int32, (2 * blk, LANES), 1) < A_HEAD_DIM
    for g in range(A_KV_HEADS):
        c0 = g * 2 * LANES
        scaled = []
        for half in range(2):
            col = (2 * g + half) * LANES
            o = _dot(p[g, half], v_all[:, col:col + LANES])
            ones_lane = _ones_lane(half)
            denom = o[:, ones_lane:ones_lane + 1] + sink_term[g, half]
            scaled.append(o * (1.0 / denom))
        acc = jnp.where(lo, scaled[0], scaled[1])
        o_ref[:, c0:c0 + LANES] = acc[:blk].astype(BF16)
        o_ref[:, c0 + LANES:c0 + 2 * LANES] = acc[blk:].astype(BF16)
    yield


def _attn_kernel(sink_ref, q_ref, kp_ref, kc_ref, kn_ref, kx_ref, vp_ref, vc_ref, vn_ref, vx_ref, o_ref,
                 *, n_ctx_blk, n_blk):
    j = pl.program_id(1)
    blk = A_BLOCK
    off = jnp.int32(4 * blk)
    cur_off = jnp.where(j >= n_ctx_blk, 0, off)
    prev_off = jnp.where(j >= n_ctx_blk + 1, 0, off)
    next_off = jnp.where((j >= n_ctx_blk) & (j <= n_blk - 2), 0, off)
    r = lax.broadcasted_iota(jnp.int32, (blk, blk), 0)
    c = lax.broadcasted_iota(jnp.int32, (blk, blk), 1)
    neg = -jnp.inf
    bias = jnp.concatenate([
        jnp.where(c >= r + prev_off, 0.0, neg),
        jnp.where(c >= cur_off, 0.0, neg),
        jnp.where(c <= r - next_off, 0.0, neg),
        jnp.zeros((blk, kx_ref.shape[1]), F32)], axis=1)
    bias = jnp.concatenate([bias, bias], axis=0)
    _run_staged([
        _attn_sample(sink_ref, q_ref.at[g], [ref.at[g] for ref in (kp_ref, kc_ref, kn_ref, kx_ref)],
                     [ref.at[g] for ref in (vp_ref, vc_ref, vn_ref, vx_ref)], o_ref.at[g], bias)
        for g in range(q_ref.shape[0])])


def _attention(sink, q, k4, v4, *, n_batch, rows_per_batch, ctx_len):
    blk, grp = A_BLOCK, ATTN_BATCH_GROUP
    n_blk = rows_per_batch // blk
    n_tok, width = q.shape
    q, k4, v4 = (_per_sample(a, n_batch) for a in (q, k4, v4))
    cur = lambda b, j: (b, j, 0)
    prev = lambda b, j: (b, jnp.maximum(j - 1, 0), 0)
    nxt = lambda b, j: (b, jnp.minimum(j + 1, n_blk - 1), 0)
    ctx = lambda b, j: (b, 0, 0)
    kv_specs = [pl.BlockSpec((grp, blk, width), prev), pl.BlockSpec((grp, blk, width), cur),
                pl.BlockSpec((grp, blk, width), nxt), pl.BlockSpec((grp, ctx_len, width), ctx)]
    out = pl.pallas_call(
        functools.partial(_attn_kernel, n_ctx_blk=ctx_len // blk, n_blk=n_blk),
        grid=(n_batch // grp, n_blk),
        in_specs=([pl.BlockSpec(memory_space=pltpu.SMEM), pl.BlockSpec((grp, blk, width), cur)]
                  + kv_specs + kv_specs),
        out_specs=pl.BlockSpec((grp, blk, width), cur),
        out_shape=jax.ShapeDtypeStruct((n_batch, rows_per_batch, width), BF16),
        compiler_params=pltpu.CompilerParams(dimension_semantics=("parallel", "parallel"),
                                             vmem_limit_bytes=VMEM_LIMIT_BYTES),
        name="attention",
    )(sink, q, k4, k4, k4, k4, v4, v4, v4, v4)
    return out.reshape(n_tok, width)


def _gla_direction(q_ref, k_ref, v_ref, la_ref, o_ref, st_ref, *, reverse):
    n = q_ref.shape[0]
    r = lax.broadcasted_iota(jnp.int32, (n, n), 0)
    c = lax.broadcasted_iota(jnp.int32, (n, n), 1)
    keep = (c >= r) if reverse else (c <= r)
    last = 0 if reverse else n - 1
    tri = jnp.where(keep, 1.0, 0.0).astype(BF16)
    cum = _cumdot(tri, la_ref[...])
    mid = cum[n // 2:n // 2 + 1, :]
    end = cum[last:last + 1, :]
    qf, kf = q_ref[...].astype(F32), k_ref[...].astype(F32)
    q_in = (qf * jnp.exp(cum - mid)).astype(BF16)
    k_in = (kf * jnp.exp(mid - cum)).astype(BF16)
    q_st = (qf * jnp.exp(cum)).astype(BF16)
    k_st = (kf * jnp.exp(end - cum)).astype(BF16)
    decay = jnp.exp(end)
    lane = lax.broadcasted_iota(jnp.int32, (n, LANES), 1)
    lo = lane < G_DK
    zero = jnp.zeros((), BF16)
    rr = lax.broadcasted_iota(jnp.int32, (2 * G_DV, 2 * G_DK), 0)
    cc = lax.broadcasted_iota(jnp.int32, (2 * G_DV, 2 * G_DK), 1)
    diag = (rr < G_DV) == (cc < G_DK)
    slabs = range(G_HEADS // 2)
    sls = [slice(s * LANES, (s + 1) * LANES) for s in slabs]
    yield
    a = []
    for s in slabs:
        qs, ks = q_in[:, sls[s]], k_in[:, sls[s]]
        a.append((jnp.where(keep, _dot_nt(qs, jnp.where(lo, ks, zero)), 0.0).astype(BF16),
                  jnp.where(keep, _dot_nt(qs, jnp.where(lo, zero, ks)), 0.0).astype(BF16)))
    yield
    for s in slabs:
        inter = _dot_nt(q_st[:, sls[s]], st_ref[s].astype(BF16))
        for half in range(2):
            vsl = slice((2 * s + half) * G_DV, (2 * s + half + 1) * G_DV)
            o_ref[:, vsl] = (_dot(a[s][half], v_ref[:, vsl])
                             + inter[:, half * G_DV:(half + 1) * G_DV]).astype(o_ref.dtype)
    yield
    for s in slabs:
        upd = _dot_tn(v_ref[:, 2 * s * G_DV:(2 * s + 2) * G_DV], k_st[:, sls[s]])
        st_ref[s] = decay[:, sls[s]] * st_ref[s] + jnp.where(diag, upd, 0.0)
    yield


def _run_staged(generators):
    for _ in zip(*generators):
        pass


def _gla_kernel(qf_ref, kf_ref, vf_ref, laf_ref, qb_ref, kb_ref, vb_ref, lab_ref, of_ref, ob_ref, sf_ref, sb_ref):
    @pl.when(pl.program_id(1) == 0)
    def _():
        sf_ref[...] = jnp.zeros_like(sf_ref)
        sb_ref[...] = jnp.zeros_like(sb_ref)

    chains = []
    for g in range(qf_ref.shape[0]):
        chains.append(_gla_direction(qf_ref.at[g], kf_ref.at[g], vf_ref.at[g], laf_ref.at[g], of_ref.at[g],
                                     sf_ref.at[g], reverse=False))
        chains.append(_gla_direction(qb_ref.at[g], kb_ref.at[g], vb_ref.at[g], lab_ref.at[g], ob_ref.at[g],
                                     sb_ref.at[g], reverse=True))
    _run_staged(chains)


def _scan_maps(n_chunks, n_ctx_chunks):
    def fwd(i):
        return i

    def bwd(i):
        return jnp.where(i < n_ctx_chunks, n_ctx_chunks - 1 - i, n_chunks - 1 + n_ctx_chunks - i)

    return fwd, bwd


def _per_sample(a, n_batch):
    return a.reshape(n_batch, a.shape[0] // n_batch, a.shape[1])


def _gla(gq, gk, gv, la, *, n_batch, rows_per_batch, ctx_len):
    n, grp = GLA_CHUNK, SCAN_BATCH_GROUP
    n_chunks = rows_per_batch // n
    fwd, bwd = _scan_maps(n_chunks, ctx_len // n)
    n_tok = gq.shape[0]
    gq, gk, gv, la = (_per_sample(a, n_batch) for a in (gq, gk, gv, la))

    def specs(chunk, la_col):
        return [pl.BlockSpec((grp, n, G_QK), lambda b, i: (b, chunk(i), 0)),
                pl.BlockSpec((grp, n, G_QK), lambda b, i: (b, chunk(i), 0)),
                pl.BlockSpec((grp, n, G_V), lambda b, i: (b, chunk(i), 0)),
                pl.BlockSpec((grp, n, G_QK), lambda b, i: (b, chunk(i), la_col))]

    state = pltpu.VMEM((grp, G_HEADS // 2, 2 * G_DV, 2 * G_DK), F32)
    out_f, out_b = pl.pallas_call(
        _gla_kernel,
        grid=(n_batch // grp, n_chunks),
        in_specs=specs(fwd, 0) + specs(bwd, 1),
        out_specs=[pl.BlockSpec((grp, n, G_V), lambda b, i: (b, fwd(i), 0)),
                   pl.BlockSpec((grp, n, G_V), lambda b, i: (b, bwd(i), 0))],
        out_shape=[jax.ShapeDtypeStruct((n_batch, rows_per_batch, G_V), BF16)] * 2,
        scratch_shapes=[state, state],
        compiler_params=pltpu.CompilerParams(dimension_semantics=("arbitrary", "arbitrary"),
                                             vmem_limit_bytes=VMEM_LIMIT_BYTES),
        name="gla",
    )(gq, gk, gv, la, gq, gk, gv, la)
    return out_f.reshape(n_tok, G_V), out_b.reshape(n_tok, G_V)


def _cumdot_right(x, tri):
    hi = x.astype(BF16)
    r1 = x - hi.astype(F32)
    mid = r1.astype(BF16)
    lo = (r1 - mid.astype(F32)).astype(BF16)
    return _dot(hi, tri) + _dot(mid, tri) + _dot(lo, tri)


def _mlstm_direction(q_ref, k_ref, v_ref, small_ref, o_ref, ct_ref, n_ref, m_ref, *, direction):
    n = q_ref.shape[0]
    reverse = direction == 1
    n_gates = 2 * M_HEADS
    jj = lax.broadcasted_iota(jnp.int32, (n, n), 0)
    ii = lax.broadcasted_iota(jnp.int32, (n, n), 1)
    vis = (jj >= ii) if reverse else (jj <= ii)
    last = 0 if reverse else n - 1
    vis_bf = jnp.where(vis, 1.0, 0.0).astype(BF16)
    gates_t = small_ref[...].T[SMALL_MI:SMALL_MI + 2 * n_gates, :]
    b_rows = _cumdot_right(gates_t, vis_bf)[n_gates:, :]
    u_rows = gates_t[:n_gates, :] - b_rows
    m_prev = m_ref[direction][:, 0:1]
    lane = lax.broadcasted_iota(jnp.int32, (n_gates, n), 1)
    m_rel = u_rows
    for step in range(n.bit_length() - 1):
        s = 1 << step
        if reverse:
            shifted = jnp.where(lane < n - s, pltpu.roll(m_rel, n - s, 1), -jnp.inf)
        else:
            shifted = jnp.where(lane >= s, pltpu.roll(m_rel, s, 1), -jnp.inf)
        m_rel = jnp.maximum(m_rel, shifted)
    m_rel = jnp.maximum(m_rel, m_prev)
    w_inter = jnp.exp(m_prev - m_rel)
    m_rows = b_rows + m_rel
    inv_floor = jnp.exp(-m_rows)
    m_new = m_rows[:, last:last + 1]
    b_end = b_rows[:, last:last + 1]
    decay = jnp.exp(b_end + m_prev - m_new)
    m_ref[direction] = jnp.broadcast_to(m_new, m_ref.shape[1:])
    log2e = 1.4426950408889634
    m2_rows = m_rel * log2e
    c2 = (b_end - m_new) * log2e
    u2_cols = jnp.concatenate([u_rows * log2e, jnp.zeros((LANES - n_gates, n), F32)], axis=0).T
    v_t = v_ref[...].T
    ones = jnp.ones((8, n), BF16)
    heads = range(M_HEADS)
    sls = [slice(h * M_HEAD_DIM, (h + 1) * M_HEAD_DIM) for h in heads]
    rows = [slice(direction * M_HEADS + h, direction * M_HEADS + h + 1) for h in heads]
    u2, s_t = [], []
    for h in heads:
        u2.append(jnp.broadcast_to(u2_cols[:, rows[h]], (n, n)))
        w_t = jnp.exp2(jnp.where(vis, u2[h] - m2_rows[rows[h], :], -jnp.inf))
        s_t.append((_dot_nt(k_ref[:, sls[h]], q_ref[:, sls[h]]) * w_t).astype(BF16))
    yield
    for h in heads:
        idx = direction * M_HEADS + h
        qh = q_ref[:, sls[h]]
        nq = _dot_nt(n_ref[idx].astype(BF16), qh)[0:1, :]
        den = w_inter[rows[h], :] * nq + _dot(ones, s_t[h])[0:1, :]
        norm = 1.0 / jnp.maximum(jnp.abs(den), inv_floor[rows[h], :])
        h_t = (_dot(v_t[sls[h], :], s_t[h])
               + _dot_nt(ct_ref[idx].astype(BF16), qh) * w_inter[rows[h], :]) * norm
        o_ref[:, sls[h]] = h_t.astype(o_ref.dtype).T
    yield
    for h in heads:
        idx = direction * M_HEADS + h
        kw = k_ref[:, sls[h]].astype(F32) * jnp.exp2(u2[h][:, :M_HEAD_DIM] + c2[rows[h], :])
        ct_ref[idx] = decay[rows[h], :] * ct_ref[idx] + _dot(v_t[sls[h], :], kw.astype(BF16))
        n_new = decay[rows[h], :] * n_ref[idx][0:1, :] + jnp.sum(kw, axis=0, keepdims=True)
        n_ref[idx] = jnp.broadcast_to(n_new, n_ref.shape[1:])
    yield


def _mlstm_kernel(qf_ref, kf_ref, vf_ref, sf_ref, qb_ref, kb_ref, vb_ref, sb_ref, of_ref, ob_ref,
                  c_ref, n_ref, m_ref):
    @pl.when(pl.program_id(1) == 0)
    def _():
        c_ref[...] = jnp.zeros_like(c_ref)
        n_ref[...] = jnp.zeros_like(n_ref)
        m_ref[...] = jnp.zeros_like(m_ref)

    chains = []
    for g in range(qf_ref.shape[0]):
        state = (c_ref.at[g], n_ref.at[g], m_ref.at[g])
        chains.append(_mlstm_direction(qf_ref.at[g], kf_ref.at[g], vf_ref.at[g], sf_ref.at[g], of_ref.at[g],
                                       *state, direction=0))
        chains.append(_mlstm_direction(qb_ref.at[g], kb_ref.at[g], vb_ref.at[g], sb_ref.at[g], ob_ref.at[g],
                                       *state, direction=1))
    _run_staged(chains)


def _mlstm(mq, mk, mv, small, *, n_batch, rows_per_batch, ctx_len):
    n, grp = MLSTM_CHUNK, SCAN_BATCH_GROUP
    n_chunks = rows_per_batch // n
    fwd, bwd = _scan_maps(n_chunks, ctx_len // n)
    n_tok = mq.shape[0]
    mq, mk, mv, small = (_per_sample(a, n_batch) for a in (mq, mk, mv, small))

    def specs(chunk):
        return ([pl.BlockSpec((grp, n, M_W), lambda b, i: (b, chunk(i), 0))] * 3
                + [pl.BlockSpec((grp, n, LANES), lambda b, i: (b, chunk(i), 0))])

    out_f, out_b = pl.pallas_call(
        _mlstm_kernel,
        grid=(n_batch // grp, n_chunks),
        in_specs=specs(fwd) + specs(bwd),
        out_specs=[pl.BlockSpec((grp, n, M_W), lambda b, i: (b, fwd(i), 0)),
                   pl.BlockSpec((grp, n, M_W), lambda b, i: (b, bwd(i), 0))],
        out_shape=[jax.ShapeDtypeStruct((n_batch, rows_per_batch, M_W), BF16)] * 2,
        scratch_shapes=[pltpu.VMEM((grp, 2 * M_HEADS, M_HEAD_DIM, M_HEAD_DIM), F32),
                        pltpu.VMEM((grp, 2 * M_HEADS, 8, M_HEAD_DIM), F32),
                        pltpu.VMEM((grp, 2, 2 * M_HEADS, LANES), F32)],
        compiler_params=pltpu.CompilerParams(dimension_semantics=("arbitrary", "arbitrary"),
                                             vmem_limit_bytes=VMEM_LIMIT_BYTES),
        name="mlstm",
    )(mq, mk, mv, small, mq, mk, mv, small)
    return out_f.reshape(n_tok, M_W), out_b.reshape(n_tok, M_W)


def _headnorm128(x, gain):
    parts = [_rmsnorm(x[:, s * LANES:(s + 1) * LANES], gain) for s in range(x.shape[1] // LANES)]
    return jnp.concatenate(parts, axis=1)


def _merge_ffn_kernel(x_ref, att_ref, gf_ref, gb_ref, hf_ref, hb_ref, gr_ref, mo_ref, gates_ref,
                      mod_ref, gn_ref, mn_ref, wa_ref, wg_ref, wm_ref, wo_ref, g_ref, w13_ref, w2_ref,
                      o_ref, h_ref, acc_ref, *, tm, ctx_len, row_base):
    b, t = pl.program_id(0), pl.program_id(1)
    d = x_ref.shape[1]
    row = _tile_rows(t, tm, row_base)
    gla = gf_ref[...].astype(F32) + gb_ref[...].astype(F32)
    g = (_headnorm128(gla, gn_ref[...]) * gr_ref[...].astype(F32)).astype(BF16)
    mls = hf_ref[...].astype(F32) + hb_ref[...].astype(F32)
    m = (_headnorm128(mls, mn_ref[...]) * mo_ref[...].astype(F32)).astype(BF16)
    y = (gates_ref[:, 0:d].astype(F32) * _dot(att_ref[...], wa_ref[...])
         + gates_ref[:, d:2 * d].astype(F32) * _dot(g, wg_ref[...])
         + gates_ref[:, 2 * d:3 * d].astype(F32) * _dot(m, wm_ref[...]))
    _, _, gate = _mod_rows(mod_ref, 1, b, row, ctx_len)
    x = x_ref[...] + gate * _dot(y.astype(BF16), wo_ref[...])
    first_row = row_base if row_base >= ctx_len else row_base + t * tm
    _adaln_tile(h_ref, x, g_ref[...], mod_ref, 2, b, row, ctx_len, first_row)
    _, _, gate = _mod_rows(mod_ref, 2, b, row, ctx_len)
    _swiglu(h_ref, w13_ref, w2_ref, acc_ref)
    o_ref[...] = x + (0.5 * gate) * acc_ref[...]


def _merge_ffn(streams, consts, *, n_batch, rows_per_batch, ctx_len, latent_only):
    xs = streams[0]
    d = xs.shape[1]
    if latent_only:
        tm = MERGE_TM_LATENT
        n_rows = rows_per_batch - ctx_len
        tpb = n_rows // tm

        def tok(width):
            align = math.gcd(rows_per_batch, ctx_len, tm)
            return pl.BlockSpec((pl.Element(tm), pl.Element(width)),
                                lambda b, t: (pl.multiple_of(b * rows_per_batch + ctx_len + t * tm, align), 0))
    else:
        tm = INPROJ_TM
        n_rows = rows_per_batch
        tpb = n_rows // tm

        def tok(width):
            return pl.BlockSpec((tm, width), lambda b, t: (b * tpb + t, 0))

    return pl.pallas_call(
        functools.partial(_merge_ffn_kernel, tm=tm, ctx_len=ctx_len, row_base=ctx_len if latent_only else 0),
        grid=(n_batch, tpb),
        in_specs=[tok(a.shape[1]) for a in streams] + [_resident(a.shape) for a in consts],
        out_specs=pl.BlockSpec((tm, d), lambda b, t: (b * tpb + t, 0)),
        out_shape=jax.ShapeDtypeStruct((n_batch * n_rows, d), F32),
        scratch_shapes=[pltpu.VMEM((tm, d), BF16), pltpu.VMEM((tm, d), F32)],
        compiler_params=pltpu.CompilerParams(dimension_semantics=("parallel", "parallel"),
                                             vmem_limit_bytes=VMEM_LIMIT_BYTES),
        name="merge_ffn",
    )(*streams, *consts)


def _rope_tables(seq, ctx_len):
    rows = seq // GRID_W
    r = jnp.repeat(jnp.arange(rows), GRID_W).astype(F32)
    col = jnp.tile(jnp.arange(GRID_W), rows).astype(F32)
    n_freq = A_HEAD_DIM // 4
    inv = ROPE_BASE ** (-jnp.arange(n_freq, dtype=F32) / n_freq)
    ang = jnp.concatenate([r[:, None] * inv, col[:, None] * inv], axis=-1)
    cos, sin = jnp.cos(ang), jnp.sin(ang)
    cos = jnp.concatenate([cos, cos, cos, cos], axis=-1)
    sin = jnp.concatenate([-sin, sin, -sin, sin], axis=-1)
    cos = jnp.concatenate([jnp.ones((ctx_len, LANES), F32), cos], axis=0)
    sin = jnp.concatenate([jnp.zeros((ctx_len, LANES), F32), sin], axis=0)
    return cos, sin


def _reorder_w_in(w):
    idx = np.cumsum((0,) + IN_SPLITS)
    (aq, ak, av, gq, gk, gv, gr, gg, mq, mk, mv, mo, mi, mf, s_a, s_g, s_m) = [
        w[:, int(idx[i]):int(idx[i + 1])] for i in range(len(IN_SPLITS))]
    pad = jnp.zeros((w.shape[0], LANES - gg.shape[1] - mi.shape[1] - mf.shape[1]), w.dtype)
    return jnp.concatenate([aq, ak, av, gq, gk, gv, gr, mq, mk, mv, mo, s_a, s_g, s_m, gg, mi, mf, pad], axis=1)


def kernel(x, c, ctx, c_ctx, mod_w, mod_b, norm_g, ffn1_w13, ffn1_w2, ffn2_w13, ffn2_w2, w_in, attn_q_norm,
           attn_k_norm, attn_sink, gla_w2, gla_b, gla_norm, mlstm_conv_w, mlstm_conv_b, mlstm_ib, mlstm_fb,
           mlstm_norm, w_out_attn, w_out_gla, w_out_mlstm, w_o):
    n_batch, seq, d = x.shape
    ctx_len = ctx.shape[1]
    depth = mod_w.shape[0]
    rows_per_batch = ctx_len + seq
    assert d == D_MODEL and n_batch < MOD_ROWS
    assert rows_per_batch % FFN_TM == 0 and rows_per_batch % INPROJ_TM == 0
    assert ctx_len % MLSTM_CHUNK == 0 and seq % MLSTM_CHUNK == 0 and seq % GRID_W == 0
    assert seq % MERGE_TM_LATENT == 0 and n_batch % SCAN_BATCH_GROUP == 0 and n_batch % ATTN_BATCH_GROUP == 0
    geo = dict(n_batch=n_batch, rows_per_batch=rows_per_batch, ctx_len=ctx_len)

    xs = jnp.concatenate([ctx, x], axis=1).reshape(n_batch * rows_per_batch, d)
    cc = jnp.concatenate([c, jnp.broadcast_to(c_ctx[None], (MOD_ROWS - n_batch, d))], axis=0)
    mod_all = _modulation(cc, mod_w, mod_b)[:, :, :n_batch + 1]
    cos, sin = _rope_tables(seq, ctx_len)

    for l in range(depth):
        mod = mod_all[l]
        gains = norm_g[l]
        xs = _ffn(xs, mod, gains[0:1], ffn1_w13[l].astype(BF16), ffn1_w2[l].astype(BF16), j=0, **geo)

        wg = jnp.zeros((LANES, 2 * G_QK), F32)
        wg = wg.at[0:G_RANK, 0:G_QK].set(gla_w2[l, 0]).at[G_RANK:2 * G_RANK, G_QK:].set(gla_w2[l, 1])
        sb = jnp.zeros((1, LANES), F32)
        sb = sb.at[0, SMALL_MI:SMALL_MI + 2 * M_HEADS].set(mlstm_ib[l].reshape(-1))
        sb = sb.at[0, SMALL_MF:SMALL_MF + 2 * M_HEADS].set(mlstm_fb[l].reshape(-1))
        (q, k4, v4, gq, gk, gv, gr, mq, mk, mv, mo, gates, small, la) = _inproj(
            xs, mod, gains[1:2], _reorder_w_in(w_in[l]).astype(BF16), wg.astype(BF16),
            gla_b[l].reshape(1, 2 * G_QK), jnp.tile(attn_q_norm[l], 2)[None], jnp.tile(attn_k_norm[l], 2)[None],
            cos, sin, sb, mlstm_conv_w[l], mlstm_conv_b[l][None], **geo)

        att = _attention(attn_sink[l], q, k4, v4, **geo)
        gf, gb = _gla(gq, gk, gv, la, **geo)
        hf, hb = _mlstm(mq, mk, mv, small, **geo)
        consts = (mod, gla_norm[l][None], mlstm_norm[l][None], w_out_attn[l].astype(BF16),
                  w_out_gla[l].astype(BF16), w_out_mlstm[l].astype(BF16), w_o[l].astype(BF16),
                  gains[2:3], ffn2_w13[l].astype(BF16), ffn2_w2[l].astype(BF16))
        xs = _merge_ffn((xs, att, gf, gb, hf, hb, gr, mo, gates), consts, latent_only=l == depth - 1, **geo)

    return xs.reshape(n_batch, seq, d)
```

```python
import functools
import math

import jax
import jax.numpy as jnp
import numpy as np
from jax import lax
from jax.experimental import pallas as pl
from jax.experimental.pallas import tpu as pltpu

F32 = jnp.float32
BF16 = jnp.bfloat16

D_MODEL = 1024
GRID_W = 64
A_HEADS, A_KV_HEADS, A_HEAD_DIM = 8, 2, 64
WINDOW = A_BLOCK = 128
ROPE_BASE = 10000.0
G_HEADS, G_DK, G_DV, G_RANK, G_TAU = 4, 64, 128, 16, 16.0
M_HEADS, M_HEAD_DIM, M_CONV = 4, 128, 5
D_FF = 2816
N_MOD = 9
EPS = 1e-6
A_Q = A_HEADS * A_HEAD_DIM
A_KV = A_KV_HEADS * A_HEAD_DIM
G_QK = G_HEADS * G_DK
G_V = G_HEADS * G_DV
M_W = M_HEADS * M_HEAD_DIM
IN_SPLITS = (A_Q, A_KV, A_KV, G_QK, G_QK, G_V, G_V, 2 * G_RANK,
             M_W, M_W, M_W, M_W, 2 * M_HEADS, 2 * M_HEADS, D_MODEL, D_MODEL, D_MODEL)

LANES = 128
F32_SUBLANES = 8
BF16_SUBLANES = 16
VMEM_LIMIT_BYTES = 56 * 1024 * 1024

FFN_TM = 768
INPROJ_TM = 384
MERGE_TM_LATENT = 512
FFN_FK = 256
GLA_CHUNK = 128
MLSTM_CHUNK = 256
SCAN_BATCH_GROUP = 4
ATTN_BATCH_GROUP = 4
MOD_ROWS = 8

SMALL_GG = 0
SMALL_MI = 2 * G_RANK
SMALL_MF = SMALL_MI + 2 * M_HEADS

_OFF = {}
_o = 0
for _name, _w in (("aq", A_Q), ("akv", 2 * A_KV), ("gqk", 2 * G_QK), ("gv", G_V), ("gr", G_V),
                  ("mqk", 2 * M_W), ("mv", M_W), ("mo", M_W), ("gates", 3 * D_MODEL), ("small", LANES)):
    _OFF[_name] = (_o, _o + _w)
    _o += _w
D_IN_PAD = _o


def _dot(a, b):
    return jnp.dot(a, b, preferred_element_type=F32)


def _dot_nt(a, b):
    return lax.dot_general(a, b, (((1,), (1,)), ((), ())), preferred_element_type=F32)


def _dot_tn(a, b):
    return lax.dot_general(a, b, (((0,), (0,)), ((), ())), preferred_element_type=F32)


def _cumdot(tri, x):
    hi = x.astype(BF16)
    r1 = x - hi.astype(F32)
    mid = r1.astype(BF16)
    lo = (r1 - mid.astype(F32)).astype(BF16)
    return _dot(tri, hi) + _dot(tri, mid) + _dot(tri, lo)


def _sigmoid(x):
    return 0.5 * jnp.tanh(0.5 * x) + 0.5


def _silu(x):
    h = 0.5 * x
    return h * jnp.tanh(h) + h


def _log_sigmoid(x):
    return jnp.minimum(x, 0.0) - jnp.log(1.0 + jnp.exp(-jnp.abs(x)))


def _rmsnorm(x, gain):
    return x * lax.rsqrt(jnp.mean(x * x, axis=-1, keepdims=True) + EPS) * gain


def _tile_rows(t, tm, base=0):
    return base + t * tm + lax.broadcasted_iota(jnp.int32, (tm, 1), 0)


def _adaln(x, gain, shift, scale):
    return (_rmsnorm(x, gain) * (1.0 + scale) + shift).astype(BF16)


def _adaln_tile(h_ref, x, gain, mod_ref, j, b, row, ctx_len, first_row):
    def latent_only():
        shift = mod_ref[3 * j, pl.ds(b, 1), :]
        scale = mod_ref[3 * j + 1, pl.ds(b, 1), :]
        inv = lax.rsqrt(jnp.mean(x * x, axis=-1, keepdims=True) + EPS)
        h_ref[...] = (x * inv * (gain * (1.0 + scale)) + shift).astype(BF16)

    def mixed():
        shift, scale, _ = _mod_rows(mod_ref, j, b, row, ctx_len)
        h_ref[...] = _adaln(x, gain, shift, scale)

    if isinstance(first_row, int):
        latent_only() if first_row >= ctx_len else mixed()
    else:
        pl.when(first_row >= ctx_len)(latent_only)
        pl.when(first_row < ctx_len)(mixed)


def _mod_rows(mod_ref, j, b, row, ctx_len):
    is_ctx = row < ctx_len
    n_b = mod_ref.shape[1] - 1
    out = []
    for q in range(3):
        lat = mod_ref[3 * j + q, pl.ds(b, 1), :]
        ctx = mod_ref[3 * j + q, n_b:n_b + 1, :]
        out.append(jnp.where(is_ctx, ctx, lat))
    return out


def _mod_kernel(c_ref, w_ref, b_ref, o_ref):
    h = _silu(c_ref[...]).astype(BF16)
    o_ref[...] = _dot(h, w_ref[...].astype(BF16)) + b_ref[...]


def _modulation(cc, mod_w, mod_b):
    depth, d, _ = mod_w.shape
    rows = cc.shape[0]
    return pl.pallas_call(
        _mod_kernel,
        grid=(depth, N_MOD),
        in_specs=[
            pl.BlockSpec((rows, d), lambda l, j: (0, 0)),
            pl.BlockSpec((None, d, d), lambda l, j: (l, 0, j)),
            pl.BlockSpec((None, None, 1, d), lambda l, j: (l, j, 0, 0)),
        ],
        out_specs=pl.BlockSpec((None, None, rows, d), lambda l, j: (l, j, 0, 0)),
        out_shape=jax.ShapeDtypeStruct((depth, N_MOD, rows, d), F32),
        compiler_params=pltpu.CompilerParams(dimension_semantics=("arbitrary", "arbitrary")),
        name="modulation",
    )(cc, mod_w, mod_b.reshape(depth, N_MOD, 1, d))


def _swiglu(h_ref, w13_ref, w2_ref, acc_ref):
    for k in range(D_FF // FFN_FK):
        lo, hi = k * FFN_FK, (k + 1) * FFN_FK
        a = _dot(h_ref[...], w13_ref[:, lo:hi])
        g = _dot(h_ref[...], w13_ref[:, D_FF + lo:D_FF + hi])
        contrib = _dot((_silu(a) * g).astype(BF16), w2_ref[lo:hi, :])
        if k == 0:
            acc_ref[...] = contrib
        else:
            acc_ref[...] += contrib


def _ffn_kernel(*refs, j, tm, ctx_len, separate_streams):
    b, t = pl.program_id(0), pl.program_id(1)
    if separate_streams:
        ctx_ref, head_ref, body_ref, mod_ref, g_ref, w13_ref, w2_ref, o_ref, h_ref, acc_ref = refs
        first = jnp.concatenate([ctx_ref[...], head_ref[...]], axis=0)
        x = jnp.where(jnp.zeros((tm, 1), jnp.int32) + t == 0, first, body_ref[...])
    else:
        x_ref, mod_ref, g_ref, w13_ref, w2_ref, o_ref, h_ref, acc_ref = refs
        x = x_ref[...]
    row = _tile_rows(t, tm)
    _adaln_tile(h_ref, x, g_ref[...], mod_ref, j, b, row, ctx_len, t * tm)
    _, _, gate = _mod_rows(mod_ref, j, b, row, ctx_len)
    _swiglu(h_ref, w13_ref, w2_ref, acc_ref)
    o_ref[...] = x + (0.5 * gate) * acc_ref[...]


def _resident(shape):
    return pl.BlockSpec(shape, lambda *_: (0,) * len(shape), pipeline_mode=pl.Buffered(1))


def _ffn(tokens, mod, gain, w13, w2, *, j, n_batch, rows_per_batch, ctx_len):
    tm = FFN_TM
    tpb = rows_per_batch // tm
    separate_streams = isinstance(tokens, tuple)
    tok = pl.BlockSpec((tm, D_MODEL), lambda b, t: (b * tpb + t, 0))
    if separate_streams:
        ctx, x = tokens
        seq = rows_per_batch - ctx_len
        head = tm - ctx_len
        align = math.gcd(seq, tm, ctx_len)
        assert 0 < head <= seq
        tok_in = [
            pl.BlockSpec((ctx_len, D_MODEL), lambda b, t: (b, 0)),
            pl.BlockSpec((pl.Element(head), pl.Element(D_MODEL)), lambda b, t: (pl.multiple_of(b * seq, align), 0)),
            pl.BlockSpec((pl.Element(tm), pl.Element(D_MODEL)),
                         lambda b, t: (pl.multiple_of(b * seq + jnp.maximum(t * tm - ctx_len, 0), align), 0)),
        ]
        tokens = (ctx, x, x)
    else:
        tok_in = [tok]
        tokens = (tokens,)
    return pl.pallas_call(
        functools.partial(_ffn_kernel, j=j, tm=tm, ctx_len=ctx_len, separate_streams=separate_streams),
        grid=(n_batch, tpb),
        in_specs=tok_in + [_resident(a.shape) for a in (mod, gain, w13, w2)],
        out_specs=tok,
        out_shape=jax.ShapeDtypeStruct((n_batch * rows_per_batch, D_MODEL), F32),
        scratch_shapes=[pltpu.VMEM((tm, D_MODEL), BF16), pltpu.VMEM((tm, D_MODEL), F32)],
        compiler_params=pltpu.CompilerParams(dimension_semantics=("parallel", "parallel"),
                                             vmem_limit_bytes=VMEM_LIMIT_BYTES),
        name=f"ffn{j}",
    )(*tokens, mod, gain, w13, w2)


def _headnorm_rope(z, gain, cos, sin):
    lane = lax.broadcasted_iota(jnp.int32, z.shape, 1)
    lo = lane < A_HEAD_DIM
    sq = z * z
    s_lo = jnp.sum(jnp.where(lo, sq, 0.0), axis=-1, keepdims=True)
    s_hi = jnp.sum(jnp.where(lo, 0.0, sq), axis=-1, keepdims=True)
    inv = lax.rsqrt(jnp.where(lo, s_lo, s_hi) * (1.0 / A_HEAD_DIM) + EPS)
    y = z * inv * gain
    first_half = (lane & (A_HEAD_DIM // 2)) == 0
    partner = jnp.where(first_half, pltpu.roll(y, LANES - A_HEAD_DIM // 2, 1), pltpu.roll(y, A_HEAD_DIM // 2, 1))
    return y * cos + partner * sin


def _spread_kv(z):
    lane = lax.broadcasted_iota(jnp.int32, z.shape, 1)
    lo = lane < A_HEAD_DIM
    a0 = jnp.where(lo, z, 0.0)
    b1 = jnp.where(lo, 0.0, z)
    return [a0, pltpu.roll(a0, A_HEAD_DIM, 1), pltpu.roll(b1, A_HEAD_DIM, 1), b1]


def _ones_lane(half):
    return A_HEAD_DIM if half == 0 else 0


def _with_ones_lane(parts):
    lane = lax.broadcasted_iota(jnp.int32, parts[0].shape, 1)
    return [jnp.where(lane == _ones_lane(i % 2), 1.0, part) for i, part in enumerate(parts)]


def _inproj_kernel(x_ref, xp_ref, xn_ref, mod_ref, g_ref, w_ref, wg_ref, bg_ref, qn_ref, kn_ref, cos_ref, sin_ref,
                   sb_ref, cw_ref, cb_ref,
                   q_o, k4_o, v4_o, gq_o, gk_o, gv_o, gr_o, mq_o, mk_o, mv_o, mo_o, gates_o, small_o, la_o,
                   h_ref, e_ref, *, tm, rows_per_batch, ctx_len):
    b, t = pl.program_id(0), pl.program_id(1)
    row = _tile_rows(t, tm)
    _adaln_tile(h_ref, x_ref[...], g_ref[...], mod_ref, 1, b, row, ctx_len, t * tm)

    def proj(name, lo, hi):
        base = _OFF[name][0]
        return _dot(h_ref[...], w_ref[:, base + lo:base + hi])

    halo = xp_ref.shape[0]
    row_h = jnp.concatenate([_tile_rows(t, tm, -halo)[:halo], _tile_rows(t + 1, tm)[:halo]], axis=0)
    shift_h, scale_h, _ = _mod_rows(mod_ref, 1, b, row_h, ctx_len)
    h_halo = _adaln(jnp.concatenate([xp_ref[...], xn_ref[...]], axis=0), g_ref[...], shift_h, scale_h)
    base = _OFF["mqk"][0]
    z_halo = _dot(h_halo, w_ref[:, base:base + 2 * M_W])
    first = t * tm
    prev_ok = (first != 0) & (first != ctx_len)
    next_ok = (first + tm != ctx_len) & (first + tm != rows_per_batch)
    e_ref[0:halo, :] = z_halo[:halo] * jnp.where(prev_ok, 1.0, 0.0)
    e_ref[halo + tm:, :] = z_halo[halo:] * jnp.where(next_ok, 1.0, 0.0)
    pad = M_CONV // 2
    conv_cols = 2 * LANES

    def conv_rows(r0, n_rows, sl, edge=None):
        acc = None
        for tap in range(M_CONV):
            off = tap - pad
            term = e_ref[pl.ds(halo + r0 + off, n_rows), sl]
            if edge is not None:
                r = r0 + lax.broadcasted_iota(jnp.int32, (n_rows, 1), 0)
                term = jnp.where((r < edge) == (r + off < edge), term, 0.0)
            term = term * cw_ref[tap:tap + 1, sl]
            acc = term if acc is None else acc + term
        return _silu(acc + cb_ref[:, sl])

    def store_qk(r0, n_rows, sl, y):
        if sl.start < M_W:
            mq_o[r0:r0 + n_rows, sl] = y.astype(BF16)
        else:
            mk_o[r0:r0 + n_rows, sl.start - M_W:sl.stop - M_W] = (y * (M_HEAD_DIM ** -0.5)).astype(BF16)

    def group_mqk(c):
        sl = slice(c * conv_cols, (c + 1) * conv_cols)
        e_ref[halo:halo + tm, sl] = proj("mqk", sl.start, sl.stop)
        store_qk(0, tm, sl, conv_rows(0, tm, sl))

    cos, sin = cos_ref[...], sin_ref[...]

    def group_aq():
        z = proj("aq", 0, A_Q)
        for s in range(A_Q // LANES):
            y = _headnorm_rope(z[:, s * LANES:(s + 1) * LANES], qn_ref[...], cos, sin)
            q_o[:, s * LANES:(s + 1) * LANES] = (y * (A_HEAD_DIM ** -0.5)).astype(BF16)

    def group_akv():
        z = proj("akv", 0, 2 * A_KV)
        for s, part in enumerate(_spread_kv(_headnorm_rope(z[:, :A_KV], kn_ref[...], cos, sin))):
            k4_o[:, s * LANES:(s + 1) * LANES] = part.astype(BF16)
        for s, part in enumerate(_with_ones_lane(_spread_kv(z[:, A_KV:]))):
            v4_o[:, s * LANES:(s + 1) * LANES] = part.astype(BF16)

    def group_gqk():
        z = proj("gqk", 0, 2 * G_QK)
        gq_o[...] = (z[:, :G_QK] * (G_DK ** -0.5)).astype(BF16)
        gk_o[...] = z[:, G_QK:].astype(BF16)

    def group_gates(s):
        gates_o[:, s * 512:(s + 1) * 512] = _sigmoid(proj("gates", s * 512, (s + 1) * 512)).astype(BF16)

    def group_small():
        zs = proj("small", 0, LANES)
        lane = lax.broadcasted_iota(jnp.int32, zs.shape, 1)
        zb = zs + sb_ref[...]
        is_f = (lane >= SMALL_MF) & (lane < SMALL_MF + 2 * M_HEADS)
        small_o[...] = jnp.where(is_f, _log_sigmoid(zb), zb)
        la_o[...] = _log_sigmoid(_dot(zs.astype(BF16), wg_ref[...]) + bg_ref[...]) * (1.0 / G_TAU)

    heavy = [functools.partial(group_mqk, c) for c in range(2 * M_W // conv_cols)]
    heavy += [group_small, group_aq, group_akv]
    light = [functools.partial(group_gates, s) for s in range(3 * D_MODEL // 512)] + [group_gqk]
    for i in range(max(len(heavy), len(light))):
        if i < len(heavy):
            heavy[i]()
        if i < len(light):
            light[i]()
    gv_o[...] = proj("gv", 0, G_V).astype(BF16)
    gr_o[...] = _silu(proj("gr", 0, G_V)).astype(BF16)
    mv_o[...] = proj("mv", 0, M_W).astype(BF16)
    mo_o[...] = _sigmoid(proj("mo", 0, M_W)).astype(BF16)

    edge_tile, edge = divmod(ctx_len, tm)
    if edge:
        span = BF16_SUBLANES
        assert edge % span == 0 and span >= pad and span <= edge <= tm - span

        @pl.when(t == edge_tile)
        def _():
            for c in range(2 * M_W // conv_cols):
                sl = slice(c * conv_cols, (c + 1) * conv_cols)
                store_qk(edge - span, 2 * span, sl, conv_rows(edge - span, 2 * span, sl, edge=edge))


def _inproj(xs, mod, gain, w_in, wg, bg, qn, kn, cos, sin, sb, conv_w, conv_b, *, n_batch, rows_per_batch, ctx_len):
    tm = INPROJ_TM
    tpb = rows_per_batch // tm
    n_tok, d = xs.shape
    halo = F32_SUBLANES
    tiles_h, n_h = tm // halo, n_tok // halo

    def tok(width):
        return pl.BlockSpec((tm, width), lambda b, t: (b * tpb + t, 0))

    before = pl.BlockSpec((halo, d), lambda b, t: (jnp.maximum((b * tpb + t) * tiles_h - 1, 0), 0))
    after = pl.BlockSpec((halo, d), lambda b, t: (jnp.minimum((b * tpb + t + 1) * tiles_h, n_h - 1), 0))
    pos = pl.BlockSpec((tm, LANES), lambda b, t: (t, 0))
    widths = (A_Q, 4 * LANES, 4 * LANES, G_QK, G_QK, G_V, G_V, M_W, M_W, M_W, M_W, 3 * D_MODEL)
    out_shape = [jax.ShapeDtypeStruct((n_tok, w), BF16) for w in widths]
    out_shape += [jax.ShapeDtypeStruct((n_tok, LANES), F32), jax.ShapeDtypeStruct((n_tok, 2 * G_QK), F32)]
    out_specs = [tok(w) for w in widths] + [tok(LANES), tok(2 * G_QK)]
    consts = (mod, gain, w_in, wg, bg, qn, kn)
    return pl.pallas_call(
        functools.partial(_inproj_kernel, tm=tm, rows_per_batch=rows_per_batch, ctx_len=ctx_len),
        grid=(n_batch, tpb),
        in_specs=([tok(d), before, after] + [_resident(a.shape) for a in consts]
                  + [pos, pos, _resident(sb.shape), _resident(conv_w.shape), _resident(conv_b.shape)]),
        out_specs=out_specs,
        out_shape=out_shape,
        scratch_shapes=[pltpu.VMEM((tm, d), BF16), pltpu.VMEM((tm + 2 * halo, 2 * M_W), F32)],
        compiler_params=pltpu.CompilerParams(dimension_semantics=("parallel", "parallel"),
                                             vmem_limit_bytes=VMEM_LIMIT_BYTES),
        name="inproj",
    )(xs, xs, xs, *consts, cos, sin, sb, conv_w, conv_b)


def _attn_sample(sink_ref, q_ref, k_refs, v_refs, o_ref, bias):
    blk = A_BLOCK
    k_all = jnp.concatenate([ref[...] for ref in k_refs], axis=0)
    v_all = jnp.concatenate([ref[...] for ref in v_refs], axis=0)
    heads_per_kv = A_HEADS // A_KV_HEADS
    combos = [(g, half) for g in range(A_KV_HEADS) for half in range(2)]
    top = lax.broadcasted_iota(jnp.int32, (2 * blk, 1), 0) < blk
    p, sink_term = {}, {}
    for g, half in combos:
        c0 = g * 2 * LANES
        qg = jnp.concatenate([q_ref[:, c0:c0 + LANES], q_ref[:, c0 + LANES:c0 + 2 * LANES]], axis=0)
        col = (2 * g + half) * LANES
        s = _dot_nt(qg, k_all[:, col:col + LANES]) + bias
        sink = jnp.where(top, sink_ref[heads_per_kv * g + half], sink_ref[heads_per_kv * g + 2 + half])
        m = jnp.maximum(jnp.max(s, axis=-1, keepdims=True), sink)
        p[g, half] = jnp.exp(s - m).astype(BF16)
        sink_term[g, half] = jnp.exp(sink - m)
    yield
    lo = lax.broadcasted_iota(jnp.int32, (2 * blk, LANES), 1) < A_HEAD_DIM
    for g in range(A_KV_HEADS):
        c0 = g * 2 * LANES
        scaled = []
        for half in range(2):
            col = (2 * g + half) * LANES
            o = _dot(p[g, half], v_all[:, col:col + LANES])
            ones_lane = _ones_lane(half)
            denom = o[:, ones_lane:ones_lane + 1] + sink_term[g, half]
            scaled.append(o * (1.0 / denom))
        acc = jnp.where(lo, scaled[0], scaled[1])
        o_ref[:, c0:c0 + LANES] = acc[:blk].astype(BF16)
        o_ref[:, c0 + LANES:c0 + 2 * LANES] = acc[blk:].astype(BF16)
    yield


def _attn_kernel(sink_ref, q_ref, kp_ref, kc_ref, kn_ref, kx_ref, vp_ref, vc_ref, vn_ref, vx_ref, o_ref,
                 *, n_ctx_blk, n_blk):
    j = pl.program_id(1)
    blk = A_BLOCK
    off = jnp.int32(4 * blk)
    cur_off = jnp.where(j >= n_ctx_blk, 0, off)
    prev_off = jnp.where(j >= n_ctx_blk + 1, 0, off)
    next_off = jnp.where((j >= n_ctx_blk) & (j <= n_blk - 2), 0, off)
    r = lax.broadcasted_iota(jnp.int32, (blk, blk), 0)
    c = lax.broadcasted_iota(jnp.int32, (blk, blk), 1)
    neg = -jnp.inf
    bias = jnp.concatenate([
        jnp.where(c >= r + prev_off, 0.0, neg),
        jnp.where(c >= cur_off, 0.0, neg),
        jnp.where(c <= r - next_off, 0.0, neg),
        jnp.zeros((blk, kx_ref.shape[1]), F32)], axis=1)
    bias = jnp.concatenate([bias, bias], axis=0)
    _run_staged([
        _attn_sample(sink_ref, q_ref.at[g], [ref.at[g] for ref in (kp_ref, kc_ref, kn_ref, kx_ref)],
                     [ref.at[g] for ref in (vp_ref, vc_ref, vn_ref, vx_ref)], o_ref.at[g], bias)
        for g in range(q_ref.shape[0])])


def _attention(sink, q, k4, v4, *, n_batch, rows_per_batch, ctx_len):
    blk, grp = A_BLOCK, ATTN_BATCH_GROUP
    n_blk = rows_per_batch // blk
    n_tok, width = q.shape
    q, k4, v4 = (_per_sample(a, n_batch) for a in (q, k4, v4))
    cur = lambda b, j: (b, j, 0)
    prev = lambda b, j: (b, jnp.maximum(j - 1, 0), 0)
    nxt = lambda b, j: (b, jnp.minimum(j + 1, n_blk - 1), 0)
    ctx = lambda b, j: (b, 0, 0)
    kv_specs = [pl.BlockSpec((grp, blk, width), prev), pl.BlockSpec((grp, blk, width), cur),
                pl.BlockSpec((grp, blk, width), nxt), pl.BlockSpec((grp, ctx_len, width), ctx)]
    out = pl.pallas_call(
        functools.partial(_attn_kernel, n_ctx_blk=ctx_len // blk, n_blk=n_blk),
        grid=(n_batch // grp, n_blk),
        in_specs=([pl.BlockSpec(memory_space=pltpu.SMEM), pl.BlockSpec((grp, blk, width), cur)]
                  + kv_specs + kv_specs),
        out_specs=pl.BlockSpec((grp, blk, width), cur),
        out_shape=jax.ShapeDtypeStruct((n_batch, rows_per_batch, width), BF16),
        compiler_params=pltpu.CompilerParams(dimension_semantics=("parallel", "parallel"),
                                             vmem_limit_bytes=VMEM_LIMIT_BYTES),
        name="attention",
    )(sink, q, k4, k4, k4, k4, v4, v4, v4, v4)
    return out.reshape(n_tok, width)


def _gla_direction(q_ref, k_ref, v_ref, la_ref, o_ref, st_ref, *, reverse):
    n = q_ref.shape[0]
    r = lax.broadcasted_iota(jnp.int32, (n, n), 0)
    c = lax.broadcasted_iota(jnp.int32, (n, n), 1)
    keep = (c >= r) if reverse else (c <= r)
    last = 0 if reverse else n - 1
    tri = jnp.where(keep, 1.0, 0.0).astype(BF16)
    cum = _cumdot(tri, la_ref[...])
    mid = cum[n // 2:n // 2 + 1, :]
    end = cum[last:last + 1, :]
    qf, kf = q_ref[...].astype(F32), k_ref[...].astype(F32)
    q_in = (qf * jnp.exp(cum - mid)).astype(BF16)
    k_in = (kf * jnp.exp(mid - cum)).astype(BF16)
    q_st = (qf * jnp.exp(cum)).astype(BF16)
    k_st = (kf * jnp.exp(end - cum)).astype(BF16)
    decay = jnp.exp(end)
    lane = lax.broadcasted_iota(jnp.int32, (n, LANES), 1)
    lo = lane < G_DK
    zero = jnp.zeros((), BF16)
    rr = lax.broadcasted_iota(jnp.int32, (2 * G_DV, 2 * G_DK), 0)
    cc = lax.broadcasted_iota(jnp.int32, (2 * G_DV, 2 * G_DK), 1)
    diag = (rr < G_DV) == (cc < G_DK)
    slabs = range(G_HEADS // 2)
    sls = [slice(s * LANES, (s + 1) * LANES) for s in slabs]
    yield
    a = []
    for s in slabs:
        qs, ks = q_in[:, sls[s]], k_in[:, sls[s]]
        a.append((jnp.where(keep, _dot_nt(qs, jnp.where(lo, ks, zero)), 0.0).astype(BF16),
                  jnp.where(keep, _dot_nt(qs, jnp.where(lo, zero, ks)), 0.0).astype(BF16)))
    yield
    for s in slabs:
        inter = _dot_nt(q_st[:, sls[s]], st_ref[s].astype(BF16))
        for half in range(2):
            vsl = slice((2 * s + half) * G_DV, (2 * s + half + 1) * G_DV)
            o_ref[:, vsl] = (_dot(a[s][half], v_ref[:, vsl])
                             + inter[:, half * G_DV:(half + 1) * G_DV]).astype(o_ref.dtype)
    yield
    for s in slabs:
        upd = _dot_tn(v_ref[:, 2 * s * G_DV:(2 * s + 2) * G_DV], k_st[:, sls[s]])
        st_ref[s] = decay[:, sls[s]] * st_ref[s] + jnp.where(diag, upd, 0.0)
    yield


def _run_staged(generators):
    for _ in zip(*generators):
        pass


def _gla_kernel(qf_ref, kf_ref, vf_ref, laf_ref, qb_ref, kb_ref, vb_ref, lab_ref, of_ref, ob_ref, sf_ref, sb_ref):
    @pl.when(pl.program_id(1) == 0)
    def _():
        sf_ref[...] = jnp.zeros_like(sf_ref)
        sb_ref[...] = jnp.zeros_like(sb_ref)

    chains = []
    for g in range(qf_ref.shape[0]):
        chains.append(_gla_direction(qf_ref.at[g], kf_ref.at[g], vf_ref.at[g], laf_ref.at[g], of_ref.at[g],
                                     sf_ref.at[g], reverse=False))
        chains.append(_gla_direction(qb_ref.at[g], kb_ref.at[g], vb_ref.at[g], lab_ref.at[g], ob_ref.at[g],
                                     sb_ref.at[g], reverse=True))
    _run_staged(chains)


def _scan_maps(n_chunks, n_ctx_chunks):
    def fwd(i):
        return i

    def bwd(i):
        return jnp.where(i < n_ctx_chunks, n_ctx_chunks - 1 - i, n_chunks - 1 + n_ctx_chunks - i)

    return fwd, bwd


def _per_sample(a, n_batch):
    return a.reshape(n_batch, a.shape[0] // n_batch, a.shape[1])


def _gla(gq, gk, gv, la, *, n_batch, rows_per_batch, ctx_len):
    n, grp = GLA_CHUNK, SCAN_BATCH_GROUP
    n_chunks = rows_per_batch // n
    fwd, bwd = _scan_maps(n_chunks, ctx_len // n)
    n_tok = gq.shape[0]
    gq, gk, gv, la = (_per_sample(a, n_batch) for a in (gq, gk, gv, la))

    def specs(chunk, la_col):
        return [pl.BlockSpec((grp, n, G_QK), lambda b, i: (b, chunk(i), 0)),
                pl.BlockSpec((grp, n, G_QK), lambda b, i: (b, chunk(i), 0)),
                pl.BlockSpec((grp, n, G_V), lambda b, i: (b, chunk(i), 0)),
                pl.BlockSpec((grp, n, G_QK), lambda b, i: (b, chunk(i), la_col))]

    state = pltpu.VMEM((grp, G_HEADS // 2, 2 * G_DV, 2 * G_DK), F32)
    out_f, out_b = pl.pallas_call(
        _gla_kernel,
        grid=(n_batch // grp, n_chunks),
        in_specs=specs(fwd, 0) + specs(bwd, 1),
        out_specs=[pl.BlockSpec((grp, n, G_V), lambda b, i: (b, fwd(i), 0)),
                   pl.BlockSpec((grp, n, G_V), lambda b, i: (b, bwd(i), 0))],
        out_shape=[jax.ShapeDtypeStruct((n_batch, rows_per_batch, G_V), BF16)] * 2,
        scratch_shapes=[state, state],
        compiler_params=pltpu.CompilerParams(dimension_semantics=("arbitrary", "arbitrary"),
                                             vmem_limit_bytes=VMEM_LIMIT_BYTES),
        name="gla",
    )(gq, gk, gv, la, gq, gk, gv, la)
    return out_f.reshape(n_tok, G_V), out_b.reshape(n_tok, G_V)


def _cumdot_right(x, tri):
    hi = x.astype(BF16)
    r1 = x - hi.astype(F32)
    mid = r1.astype(BF16)
    lo = (r1 - mid.astype(F32)).astype(BF16)
    return _dot(hi, tri) + _dot(mid, tri) + _dot(lo, tri)


def _mlstm_direction(q_ref, k_ref, v_ref, small_ref, o_ref, ct_ref, n_ref, m_ref, *, direction):
    n = q_ref.shape[0]
    reverse = direction == 1
    n_gates = 2 * M_HEADS
    jj = lax.broadcasted_iota(jnp.int32, (n, n), 0)
    ii = lax.broadcasted_iota(jnp.int32, (n, n), 1)
    vis = (jj >= ii) if reverse else (jj <= ii)
    last = 0 if reverse else n - 1
    vis_bf = jnp.where(vis, 1.0, 0.0).astype(BF16)
    gates_t = small_ref[...].T[SMALL_MI:SMALL_MI + 2 * n_gates, :]
    b_rows = _cumdot_right(gates_t, vis_bf)[n_gates:, :]
    u_rows = gates_t[:n_gates, :] - b_rows
    m_prev = m_ref[direction][:, 0:1]
    lane = lax.broadcasted_iota(jnp.int32, (n_gates, n), 1)
    m_rel = u_rows
    for step in range(n.bit_length() - 1):
        s = 1 << step
        if reverse:
            shifted = jnp.where(lane < n - s, pltpu.roll(m_rel, n - s, 1), -jnp.inf)
        else:
            shifted = jnp.where(lane >= s, pltpu.roll(m_rel, s, 1), -jnp.inf)
        m_rel = jnp.maximum(m_rel, shifted)
    m_rel = jnp.maximum(m_rel, m_prev)
    w_inter = jnp.exp(m_prev - m_rel)
    m_rows = b_rows + m_rel
    inv_floor = jnp.exp(-m_rows)
    m_new = m_rows[:, last:last + 1]
    b_end = b_rows[:, last:last + 1]
    decay = jnp.exp(b_end + m_prev - m_new)
    m_ref[direction] = jnp.broadcast_to(m_new, m_ref.shape[1:])
    log2e = 1.4426950408889634
    m2_rows = m_rel * log2e
    c2 = (b_end - m_new) * log2e
    u2_cols = jnp.concatenate([u_rows * log2e, jnp.zeros((LANES - n_gates, n), F32)], axis=0).T
    v_t = v_ref[...].T
    ones = jnp.ones((8, n), BF16)
    heads = range(M_HEADS)
    sls = [slice(h * M_HEAD_DIM, (h + 1) * M_HEAD_DIM) for h in heads]
    rows = [slice(direction * M_HEADS + h, direction * M_HEADS + h + 1) for h in heads]
    u2, s_t = [], []
    for h in heads:
        u2.append(jnp.broadcast_to(u2_cols[:, rows[h]], (n, n)))
        w_t = jnp.exp2(jnp.where(vis, u2[h] - m2_rows[rows[h], :], -jnp.inf))
        s_t.append((_dot_nt(k_ref[:, sls[h]], q_ref[:, sls[h]]) * w_t).astype(BF16))
    yield
    for h in heads:
        idx = direction * M_HEADS + h
        qh = q_ref[:, sls[h]]
        nq = _dot_nt(n_ref[idx].astype(BF16), qh)[0:1, :]
        den = w_inter[rows[h], :] * nq + _dot(ones, s_t[h])[0:1, :]
        norm = 1.0 / jnp.maximum(jnp.abs(den), inv_floor[rows[h], :])
        h_t = (_dot(v_t[sls[h], :], s_t[h])
               + _dot_nt(ct_ref[idx].astype(BF16), qh) * w_inter[rows[h], :]) * norm
        o_ref[:, sls[h]] = h_t.astype(o_ref.dtype).T
    yield
    for h in heads:
        idx = direction * M_HEADS + h
        kw = k_ref[:, sls[h]].astype(F32) * jnp.exp2(u2[h][:, :M_HEAD_DIM] + c2[rows[h], :])
        ct_ref[idx] = decay[rows[h], :] * ct_ref[idx] + _dot(v_t[sls[h], :], kw.astype(BF16))
        n_new = decay[rows[h], :] * n_ref[idx][0:1, :] + jnp.sum(kw, axis=0, keepdims=True)
        n_ref[idx] = jnp.broadcast_to(n_new, n_ref.shape[1:])
    yield


def _mlstm_kernel(qf_ref, kf_ref, vf_ref, sf_ref, qb_ref, kb_ref, vb_ref, sb_ref, of_ref, ob_ref,
                  c_ref, n_ref, m_ref):
    @pl.when(pl.program_id(1) == 0)
    def _():
        c_ref[...] = jnp.zeros_like(c_ref)
        n_ref[...] = jnp.zeros_like(n_ref)
        m_ref[...] = jnp.zeros_like(m_ref)

    chains = []
    for g in range(qf_ref.shape[0]):
        state = (c_ref.at[g], n_ref.at[g], m_ref.at[g])
        chains.append(_mlstm_direction(qf_ref.at[g], kf_ref.at[g], vf_ref.at[g], sf_ref.at[g], of_ref.at[g],
                                       *state, direction=0))
        chains.append(_mlstm_direction(qb_ref.at[g], kb_ref.at[g], vb_ref.at[g], sb_ref.at[g], ob_ref.at[g],
                                       *state, direction=1))
    _run_staged(chains)


def _mlstm(mq, mk, mv, small, *, n_batch, rows_per_batch, ctx_len):
    n, grp = MLSTM_CHUNK, SCAN_BATCH_GROUP
    n_chunks = rows_per_batch // n
    fwd, bwd = _scan_maps(n_chunks, ctx_len // n)
    n_tok = mq.shape[0]
    mq, mk, mv, small = (_per_sample(a, n_batch) for a in (mq, mk, mv, small))

    def specs(chunk):
        return ([pl.BlockSpec((grp, n, M_W), lambda b, i: (b, chunk(i), 0))] * 3
                + [pl.BlockSpec((grp, n, LANES), lambda b, i: (b, chunk(i), 0))])

    out_f, out_b = pl.pallas_call(
        _mlstm_kernel,
        grid=(n_batch // grp, n_chunks),
        in_specs=specs(fwd) + specs(bwd),
        out_specs=[pl.BlockSpec((grp, n, M_W), lambda b, i: (b, fwd(i), 0)),
                   pl.BlockSpec((grp, n, M_W), lambda b, i: (b, bwd(i), 0))],
        out_shape=[jax.ShapeDtypeStruct((n_batch, rows_per_batch, M_W), BF16)] * 2,
        scratch_shapes=[pltpu.VMEM((grp, 2 * M_HEADS, M_HEAD_DIM, M_HEAD_DIM), F32),
                        pltpu.VMEM((grp, 2 * M_HEADS, 8, M_HEAD_DIM), F32),
                        pltpu.VMEM((grp, 2, 2 * M_HEADS, LANES), F32)],
        compiler_params=pltpu.CompilerParams(dimension_semantics=("arbitrary", "arbitrary"),
                                             vmem_limit_bytes=VMEM_LIMIT_BYTES),
        name="mlstm",
    )(mq, mk, mv, small, mq, mk, mv, small)
    return out_f.reshape(n_tok, M_W), out_b.reshape(n_tok, M_W)


def _headnorm128(x, gain):
    parts = [_rmsnorm(x[:, s * LANES:(s + 1) * LANES], gain) for s in range(x.shape[1] // LANES)]
    return jnp.concatenate(parts, axis=1)


def _merge_ffn_kernel(x_ref, att_ref, gf_ref, gb_ref, hf_ref, hb_ref, gr_ref, mo_ref, gates_ref,
                      mod_ref, gn_ref, mn_ref, wa_ref, wg_ref, wm_ref, wo_ref, g_ref, w13_ref, w2_ref,
                      o_ref, h_ref, acc_ref, *, tm, ctx_len, row_base):
    b, t = pl.program_id(0), pl.program_id(1)
    d = x_ref.shape[1]
    row = _tile_rows(t, tm, row_base)
    gla = gf_ref[...].astype(F32) + gb_ref[...].astype(F32)
    g = (_headnorm128(gla, gn_ref[...]) * gr_ref[...].astype(F32)).astype(BF16)
    mls = hf_ref[...].astype(F32) + hb_ref[...].astype(F32)
    m = (_headnorm128(mls, mn_ref[...]) * mo_ref[...].astype(F32)).astype(BF16)
    y = (gates_ref[:, 0:d].astype(F32) * _dot(att_ref[...], wa_ref[...])
         + gates_ref[:, d:2 * d].astype(F32) * _dot(g, wg_ref[...])
         + gates_ref[:, 2 * d:3 * d].astype(F32) * _dot(m, wm_ref[...]))
    _, _, gate = _mod_rows(mod_ref, 1, b, row, ctx_len)
    x = x_ref[...] + gate * _dot(y.astype(BF16), wo_ref[...])
    first_row = row_base if row_base >= ctx_len else row_base + t * tm
    _adaln_tile(h_ref, x, g_ref[...], mod_ref, 2, b, row, ctx_len, first_row)
    _, _, gate = _mod_rows(mod_ref, 2, b, row, ctx_len)
    _swiglu(h_ref, w13_ref, w2_ref, acc_ref)
    o_ref[...] = x + (0.5 * gate) * acc_ref[...]


def _merge_ffn(streams, consts, *, n_batch, rows_per_batch, ctx_len, latent_only):
    xs = streams[0]
    d = xs.shape[1]
    if latent_only:
        tm = MERGE_TM_LATENT
        n_rows = rows_per_batch - ctx_len
        tpb = n_rows // tm

        def tok(width):
            align = math.gcd(rows_per_batch, ctx_len, tm)
            return pl.BlockSpec((pl.Element(tm), pl.Element(width)),
                                lambda b, t: (pl.multiple_of(b * rows_per_batch + ctx_len + t * tm, align), 0))
    else:
        tm = INPROJ_TM
        n_rows = rows_per_batch
        tpb = n_rows // tm

        def tok(width):
            return pl.BlockSpec((tm, width), lambda b, t: (b * tpb + t, 0))

    return pl.pallas_call(
        functools.partial(_merge_ffn_kernel, tm=tm, ctx_len=ctx_len, row_base=ctx_len if latent_only else 0),
        grid=(n_batch, tpb),
        in_specs=[tok(a.shape[1]) for a in streams] + [_resident(a.shape) for a in consts],
        out_specs=pl.BlockSpec((tm, d), lambda b, t: (b * tpb + t, 0)),
        out_shape=jax.ShapeDtypeStruct((n_batch * n_rows, d), F32),
        scratch_shapes=[pltpu.VMEM((tm, d), BF16), pltpu.VMEM((tm, d), F32)],
        compiler_params=pltpu.CompilerParams(dimension_semantics=("parallel", "parallel"),
                                             vmem_limit_bytes=VMEM_LIMIT_BYTES),
        name="merge_ffn",
    )(*streams, *consts)


def _rope_tables(seq, ctx_len):
    rows = seq // GRID_W
    r = jnp.repeat(jnp.arange(rows), GRID_W).astype(F32)
    col = jnp.tile(jnp.arange(GRID_W), rows).astype(F32)
    n_freq = A_HEAD_DIM // 4
    inv = ROPE_BASE ** (-jnp.arange(n_freq, dtype=F32) / n_freq)
    ang = jnp.concatenate([r[:, None] * inv, col[:, None] * inv], axis=-1)
    cos, sin = jnp.cos(ang), jnp.sin(ang)
    cos = jnp.concatenate([cos, cos, cos, cos], axis=-1)
    sin = jnp.concatenate([-sin, sin, -sin, sin], axis=-1)
    cos = jnp.concatenate([jnp.ones((ctx_len, LANES), F32), cos], axis=0)
    sin = jnp.concatenate([jnp.zeros((ctx_len, LANES), F32), sin], axis=0)
    return cos, sin


def _reorder_w_in(w):
    idx = np.cumsum((0,) + IN_SPLITS)
    (aq, ak, av, gq, gk, gv, gr, gg, mq, mk, mv, mo, mi, mf, s_a, s_g, s_m) = [
        w[:, int(idx[i]):int(idx[i + 1])] for i in range(len(IN_SPLITS))]
    pad = jnp.zeros((w.shape[0], LANES - gg.shape[1] - mi.shape[1] - mf.shape[1]), w.dtype)
    return jnp.concatenate([aq, ak, av, gq, gk, gv, gr, mq, mk, mv, mo, s_a, s_g, s_m, gg, mi, mf, pad], axis=1)


def kernel(x, c, ctx, c_ctx, mod_w, mod_b, norm_g, ffn1_w13, ffn1_w2, ffn2_w13, ffn2_w2, w_in, attn_q_norm,
           attn_k_norm, attn_sink, gla_w2, gla_b, gla_norm, mlstm_conv_w, mlstm_conv_b, mlstm_ib, mlstm_fb,
           mlstm_norm, w_out_attn, w_out_gla, w_out_mlstm, w_o):
    n_batch, seq, d = x.shape
    ctx_len = ctx.shape[1]
    depth = mod_w.shape[0]
    rows_per_batch = ctx_len + seq
    assert d == D_MODEL and n_batch < MOD_ROWS
    assert rows_per_batch % FFN_TM == 0 and rows_per_batch % INPROJ_TM == 0
    assert ctx_len % MLSTM_CHUNK == 0 and seq % MLSTM_CHUNK == 0 and seq % GRID_W == 0
    assert seq % MERGE_TM_LATENT == 0 and n_batch % SCAN_BATCH_GROUP == 0 and n_batch % ATTN_BATCH_GROUP == 0
    geo = dict(n_batch=n_batch, rows_per_batch=rows_per_batch, ctx_len=ctx_len)

    xs = (ctx.reshape(n_batch * ctx_len, d), x.reshape(n_batch * seq, d))
    cc = jnp.concatenate([c, jnp.broadcast_to(c_ctx[None], (MOD_ROWS - n_batch, d))], axis=0)
    mod_all = _modulation(cc, mod_w, mod_b)[:, :, :n_batch + 1]
    cos, sin = _rope_tables(seq, ctx_len)

    for l in range(depth):
        mod = mod_all[l]
        gains = norm_g[l]
        xs = _ffn(xs, mod, gains[0:1], ffn1_w13[l].astype(BF16), ffn1_w2[l].astype(BF16), j=0, **geo)

        wg = jnp.zeros((LANES, 2 * G_QK), F32)
        wg = wg.at[0:G_RANK, 0:G_QK].set(gla_w2[l, 0]).at[G_RANK:2 * G_RANK, G_QK:].set(gla_w2[l, 1])
        sb = jnp.zeros((1, LANES), F32)
        sb = sb.at[0, SMALL_MI:SMALL_MI + 2 * M_HEADS].set(mlstm_ib[l].reshape(-1))
        sb = sb.at[0, SMALL_MF:SMALL_MF + 2 * M_HEADS].set(mlstm_fb[l].reshape(-1))
        (q, k4, v4, gq, gk, gv, gr, mq, mk, mv, mo, gates, small, la) = _inproj(
            xs, mod, gains[1:2], _reorder_w_in(w_in[l]).astype(BF16), wg.astype(BF16),
            gla_b[l].reshape(1, 2 * G_QK), jnp.tile(attn_q_norm[l], 2)[None], jnp.tile(attn_k_norm[l], 2)[None],
            cos, sin, sb, mlstm_conv_w[l], mlstm_conv_b[l][None], **geo)

        att = _attention(attn_sink[l], q, k4, v4, **geo)
        gf, gb = _gla(gq, gk, gv, la, **geo)
        hf, hb = _mlstm(mq, mk, mv, small, **geo)
        consts = (mod, gla_norm[l][None], mlstm_norm[l][None], w_out_attn[l].astype(BF16),
                  w_out_gla[l].astype(BF16), w_out_mlstm[l].astype(BF16), w_o[l].astype(BF16),
                  gains[2:3], ffn2_w13[l].astype(BF16), ffn2_w2[l].astype(BF16))
        xs = _merge_ffn((xs, att, gf, gb, hf, hb, gr, mo, gates), consts, latent_only=l == depth - 1, **geo)

    return xs.reshape(n_batch, seq, d)
```
